```python
import jax, jax.numpy as jnp
from jax import lax
import numpy as np

D_MODEL = 2048
BATCH = 8
SEQ = 4096
DEPTH = 2
DEC_BATCH = 8
DEC_SEQ = 2048
PAST_LEN = 128

HEAD_DIM = 128
DIL_GROUPS = ((128, 1), (512, 4), (2048, 16))
N_GROUPS = len(DIL_GROUPS)
A_SLOTS = D_MODEL // 256
A_HEADS = N_GROUPS * A_SLOTS
RET_HEADS = D_MODEL // 256
RET_DK = D_MODEL // RET_HEADS
RET_DV = RET_DK
RET_CHUNK = 128
CROSS_HEADS = 4
CROSS_DIM = CROSS_HEADS * HEAD_DIM
N_MEM = 256
D_FF = -(-8 * D_MODEL // (3 * 256)) * 256
N_MIXERS = 2
N_A = (DEPTH + 1) // 2
N_B = DEPTH // 2
A_IN = 3 * A_HEADS * HEAD_DIM + CROSS_DIM
A_OUT = A_SLOTS * HEAD_DIM + CROSS_DIM
B_IN = 2 * RET_HEADS * RET_DK + 2 * RET_HEADS * RET_DV + CROSS_DIM
B_OUT = RET_HEADS * RET_DV + CROSS_DIM
EPS = 1e-6
F32 = jnp.float32

kernel_name = 'hybrid_dilated_retention_encoder'


def rmsnorm(x, g):
    xf = x.astype(F32)
    y = xf * lax.rsqrt(jnp.mean(xf * xf, axis=-1, keepdims=True) + EPS)
    return (y * g.astype(F32)).astype(x.dtype)


def alibi_slopes():
    j = jnp.arange(1, A_HEADS + 1, dtype=F32)
    return (2.0 ** (-8.0 * j / A_HEADS)).reshape(N_GROUPS, A_SLOTS)


def dilated_window_attention(q, k, v, window, dil, slopes):
    b, s, h, hd = q.shape
    w = window // (2 * dil)
    L = s // dil
    nb = -(-L // w)
    lp = nb * w

    def split(t):
        return t.reshape(b, L, dil, h, hd).transpose(0, 2, 3, 1, 4)

    qb = jnp.pad(split(q), ((0, 0), (0, 0), (0, 0), (0, lp - L), (0, 0))).reshape(b, dil, h, nb, w, hd)

    def band(t):
        t = jnp.pad(split(t), ((0, 0), (0, 0), (0, 0), (w, lp - L + w), (0, 0))).reshape(b, dil, h, nb + 2, w, hd)
        return jnp.concatenate([t[:, :, :, :-2], t[:, :, :, 1:-1], t[:, :, :, 2:]], axis=4)

    kb, vb = band(k), band(v)
    scores = jnp.einsum('brhnqd,brhnkd->brhnqk', qb, kb, preferred_element_type=F32) * (hd ** -0.5)
    il = jnp.arange(w)[:, None]
    c = jnp.arange(3 * w)[None, :]
    rel = c - w - il
    jidx = (jnp.arange(nb) * w)[:, None, None] + c[None] - w
    valid = (jnp.abs(rel) <= w)[None] & (jidx >= 0) & (jidx < L)
    bias = -slopes.astype(F32)[:, None, None, None] * (dil * jnp.abs(rel)).astype(F32)[None, None]
    scores = jnp.where(valid, scores + bias, -jnp.inf)
    lse = jax.nn.logsumexp(scores, axis=-1)
    p = jnp.exp(scores - lse[..., None])
    o = jnp.einsum('brhnqk,brhnkd->brhnqd', p.astype(v.dtype), vb)
    o = o.reshape(b, dil, h, lp, hd)[:, :, :, :L].transpose(0, 3, 1, 2, 4).reshape(b, s, h, hd)
    lse = lse.reshape(b, dil, h, lp)[..., :L].transpose(0, 3, 1, 2).reshape(b, s, h)
    return o, lse


def mixer_a(xn, w_in):
    b, s, _ = xn.shape
    proj = xn @ w_in
    nqkv = 3 * A_HEADS * HEAD_DIM
    qkv = proj[..., :nqkv].reshape(b, s, 3, N_GROUPS, A_SLOTS, HEAD_DIM)
    qc = proj[..., nqkv:]
    slopes = alibi_slopes()
    outs, lses = [], []
    for g, (win, dil) in enumerate(DIL_GROUPS):
        o, l = dilated_window_attention(qkv[:, :, 0, g], qkv[:, :, 1, g], qkv[:, :, 2, g], win, dil, slopes[g])
        outs.append(o)
        lses.append(l)
    wts = jax.nn.softmax(jnp.stack(lses, axis=0), axis=0)
    o = jnp.einsum('gbsh,gbshd->bshd', wts, jnp.stack(outs, axis=0).astype(F32))
    return o.reshape(b, s, A_SLOTS * HEAD_DIM).astype(xn.dtype), qc


def retention_direction(q, k, v, log_gamma, reverse):
    n, b, h, c, dk = q.shape
    dv = v.shape[-1]
    i = jnp.arange(c, dtype=F32)
    diff = i[:, None] - i[None, :]
    lg = log_gamma[:, None]
    if reverse:
        mask = diff < 0
        dist = -diff
        q_dec = jnp.exp((c - i)[None, :] * lg)
        k_dec = jnp.exp(i[None, :] * lg)
    else:
        mask = diff >= 0
        dist = diff
        q_dec = jnp.exp((i + 1.0)[None, :] * lg)
        k_dec = jnp.exp((c - 1.0 - i)[None, :] * lg)
    intra = jnp.where(mask[None], jnp.exp(jnp.where(mask, dist, 0.0)[None] * lg[:, :, None]), 0.0)
    chunk_dec = jnp.exp(c * log_gamma)[None, :, None, None]

    def step(state, qkv):
        qc, kc, vc = qkv
        sc = jnp.einsum('bhid,bhjd->bhij', qc, kc) * intra
        out = jnp.einsum('bhij,bhje->bhie', sc, vc) + jnp.einsum('bhid,bhde->bhie', qc * q_dec[:, :, None], state)
        state = chunk_dec * state + jnp.einsum('bhjd,bhje->bhde', kc * k_dec[:, :, None], vc)
        return state, out

    init = jnp.zeros((b, h, dk, dv), F32)
    _, out = lax.scan(step, init, (q, k, v), reverse=reverse)
    return out


def mixer_b(xn, w_in, e_fwd, e_bwd):
    b, s, _ = xn.shape
    n = s // RET_CHUNK
    proj = xn @ w_in
    hk, hv = RET_HEADS * RET_DK, RET_HEADS * RET_DV
    q = proj[..., :hk]
    k = proj[..., hk:2 * hk]
    v = proj[..., 2 * hk:2 * hk + hv]
    gate = proj[..., 2 * hk + hv:2 * hk + 2 * hv]
    qc = proj[..., 2 * hk + 2 * hv:]

    def chunks(t, d):
        return t.astype(F32).reshape(b, n, RET_CHUNK, RET_HEADS, d).transpose(1, 0, 3, 2, 4)

    qh = chunks(q, RET_DK) * (RET_DK ** -0.5)
    kh, vh = chunks(k, RET_DK), chunks(v, RET_DV)
    lg_f = jnp.log1p(-jnp.exp2(-e_fwd.astype(F32)))
    lg_b = jnp.log1p(-jnp.exp2(-e_bwd.astype(F32)))
    y = retention_direction(qh, kh, vh, lg_f, False) + retention_direction(qh, kh, vh, lg_b, True)
    y = y.transpose(1, 0, 3, 2, 4).reshape(b, s, RET_HEADS, RET_DV)
    y = y * lax.rsqrt(jnp.mean(y * y, axis=-1, keepdims=True) + EPS)
    y = y.reshape(b, s, hv) * jax.nn.silu(gate.astype(F32))
    return y.astype(xn.dtype), qc


def memory_cross_attention(qc, mem, g_mem, w_mem_kv):
    b, s, _ = qc.shape
    m = rmsnorm(mem, g_mem)
    kv = (m @ w_mem_kv).reshape(b, mem.shape[1], 2, CROSS_HEADS, HEAD_DIM)
    q = qc.reshape(b, s, CROSS_HEADS, HEAD_DIM)
    sc = jnp.einsum('bshd,bmhd->bhsm', q, kv[:, :, 0], preferred_element_type=F32) * (HEAD_DIM ** -0.5)
    p = jax.nn.softmax(sc, axis=-1)
    o = jnp.einsum('bhsm,bmhd->bshd', p.astype(mem.dtype), kv[:, :, 1])
    return o.reshape(b, s, CROSS_DIM)


def trunk(x, mem, norm_mix, norm_mem, w_mem_kv, a_w_in, a_w_out, b_w_in, b_w_out, b_decay_fwd, b_decay_bwd,
          norm_ffn, w_gate_up, w_down, norm_final):
    for i in range(DEPTH):
        xn = rmsnorm(x, norm_mix[i])
        j = i // N_MIXERS
        if i % N_MIXERS == 0:
            mix, qc = mixer_a(xn, a_w_in[j])
            w_out = a_w_out[j]
        else:
            mix, qc = mixer_b(xn, b_w_in[j], b_decay_fwd[j], b_decay_bwd[j])
            w_out = b_w_out[j]
        cross = memory_cross_attention(qc, mem, norm_mem[i], w_mem_kv[i])
        x = x + jnp.concatenate([mix, cross], axis=-1) @ w_out
        hdn = rmsnorm(x, norm_ffn[i]) @ w_gate_up[i]
        gate, up = jnp.split(hdn, 2, axis=-1)
        x = x + (jax.nn.silu(gate) * up) @ w_down[i]
    return rmsnorm(x, norm_final)


def setup_inputs(seed: int = 0) -> dict:
    key = jax.random.key(seed)
    ks = jax.random.split(key, 20)

    def dense(k, shape, fan_in):
        return jax.random.normal(k, shape, F32) * (fan_in ** -0.5)

    def gain(k, shape):
        return 1.0 + 0.02 * jax.random.normal(k, shape, F32)

    base = 5.0 + jnp.arange(RET_HEADS, dtype=F32)
    return {
        'x_prompt': jax.random.normal(ks[0], (BATCH, SEQ, D_MODEL), F32),
        'x_sample': jax.random.normal(ks[1], (DEC_BATCH, DEC_SEQ, D_MODEL), F32),
        'mem_prompt': jax.random.normal(ks[2], (BATCH, N_MEM, D_MODEL), F32),
        'mem_sample': jax.random.normal(ks[3], (DEC_BATCH, N_MEM, D_MODEL), F32),
        'norm_mix': gain(ks[4], (DEPTH, D_MODEL)),
        'norm_mem': gain(ks[5], (DEPTH, D_MODEL)),
        'w_mem_kv': dense(ks[6], (DEPTH, D_MODEL, 2 * CROSS_DIM), D_MODEL),
        'a_w_in': dense(ks[7], (N_A, D_MODEL, A_IN), D_MODEL),
        'a_w_out': dense(ks[8], (N_A, A_OUT, D_MODEL), A_OUT),
        'b_w_in': dense(ks[9], (N_B, D_MODEL, B_IN), D_MODEL),
        'b_w_out': dense(ks[10], (N_B, B_OUT, D_MODEL), B_OUT),
        'b_decay_fwd': base[None] + 0.1 * jax.random.normal(ks[11], (N_B, RET_HEADS), F32),
        'b_decay_bwd': base[None] + 0.5 + 0.1 * jax.random.normal(ks[12], (N_B, RET_HEADS), F32),
        'norm_ffn': gain(ks[13], (DEPTH, D_MODEL)),
        'w_gate_up': dense(ks[14], (DEPTH, D_MODEL, 2 * D_FF), D_MODEL),
        'w_down': dense(ks[15], (DEPTH, D_FF, D_MODEL), D_FF),
        'norm_final': gain(ks[16], (D_MODEL,)),
    }


def reference(x_prompt, x_sample, mem_prompt, mem_sample, norm_mix, norm_mem, w_mem_kv, a_w_in, a_w_out,
              b_w_in, b_w_out, b_decay_fwd, b_decay_bwd, norm_ffn, w_gate_up, w_down, norm_final):
    y_prompt = trunk(x_prompt, mem_prompt, norm_mix, norm_mem, w_mem_kv, a_w_in, a_w_out, b_w_in, b_w_out,
                     b_decay_fwd, b_decay_bwd, norm_ffn, w_gate_up, w_down, norm_final)
    y_sample = trunk(x_sample, mem_sample, norm_mix, norm_mem, w_mem_kv, a_w_in, a_w_out, b_w_in, b_w_out,
                     b_decay_fwd, b_decay_bwd, norm_ffn, w_gate_up, w_down, norm_final)
    return (y_prompt, y_sample)
```

```python
import functools

import jax
import jax.numpy as jnp
from jax import lax
from jax.experimental import pallas as pl
from jax.experimental.pallas import tpu as pltpu

F32 = jnp.float32
BF16 = jnp.bfloat16

D_MODEL = 2048
HEAD_DIM = 128
DIL_GROUPS = ((128, 1), (512, 4), (2048, 16))
N_GROUPS = len(DIL_GROUPS)
A_SLOTS = D_MODEL // 256
A_HEADS = N_GROUPS * A_SLOTS
RET_HEADS = D_MODEL // 256
RET_DK = 256
RET_CHUNK = 128
CROSS_HEADS = 4
CROSS_DIM = CROSS_HEADS * HEAD_DIM
N_MEM = 256
D_FF = 5632
EPS = 1e-6
NEG_BIG = -1e30

ATT_BLOCK = 128
ATT_HALF = 64
ATT_KEYS = 256

V7X_VMEM_LIMIT = 56 * 1024 * 1024


def _params(semantics):
    return pltpu.CompilerParams(dimension_semantics=semantics, vmem_limit_bytes=V7X_VMEM_LIMIT)


def _rmsnorm_f32(x, g):
    ms = jnp.mean(x * x, axis=-1, keepdims=True)
    return (x * lax.rsqrt(ms + EPS)) * g


def _norm_proj_kernel(x_ref, g_ref, w_ref, o_ref, xn_ref, *, heads_per_step, head_w):
    @pl.when(pl.program_id(2) == 0)
    def _():
        xn_ref[...] = _rmsnorm_f32(x_ref[0], g_ref[...]).astype(BF16)

    res = jnp.dot(xn_ref[...], w_ref[...], preferred_element_type=F32)
    for h in range(heads_per_step):
        o_ref[0, h] = res[:, h * head_w:(h + 1) * head_w].astype(o_ref.dtype)


def _norm_proj(x, g, w, *, head_w, tm, tn):
    b, s, d = x.shape
    n = w.shape[1]
    heads_per_step = tn // head_w
    kern = functools.partial(_norm_proj_kernel, heads_per_step=heads_per_step, head_w=head_w)
    return pl.pallas_call(
        kern,
        grid=(b, s // tm, n // tn),
        in_specs=[
            pl.BlockSpec((1, tm, d), lambda bi, i, j: (bi, i, 0)),
            pl.BlockSpec((1, d), lambda bi, i, j: (0, 0)),
            pl.BlockSpec((d, tn), lambda bi, i, j: (0, j)),
        ],
        out_specs=pl.BlockSpec((1, heads_per_step, tm, head_w), lambda bi, i, j: (bi, j, i, 0)),
        out_shape=jax.ShapeDtypeStruct((b, n // head_w, s, head_w), BF16),
        scratch_shapes=[pltpu.VMEM((tm, d), BF16)],
        compiler_params=_params(("parallel", "parallel", "arbitrary")),
        name="norm_proj",
    )(x, g.reshape(1, d), w)


def _dil_attn_kernel(q0, k0, v0, q1, k1, v1, q2, k2, v2, o_ref, acc_ref, m_ref, l_ref, *, seq):
    h = pl.program_id(1)
    scale = HEAD_DIM ** -0.5
    groups = ((q0, k0, v0), (q1, k1, v1), (q2, k2, v2))

    def run_group(g, first):
        q_ref, k_ref, v_ref = groups[g]
        dil = DIL_GROUPS[g][1]
        length = seq // dil
        kw = min(ATT_KEYS, length)
        nblk = length // ATT_BLOCK
        jf = (h + (g * A_SLOTS + 1)).astype(F32)
        slope = jnp.exp2(jnp.full((1, kw), -8.0 / A_HEADS, F32) * jf)
        row = lax.broadcasted_iota(jnp.int32, (ATT_BLOCK, kw), 0)
        col = lax.broadcasted_iota(jnp.int32, (ATT_BLOCK, kw), 1)
        col_minus_row = col - row

        for r in range(dil):
            lanes = slice(r * HEAD_DIM, (r + 1) * HEAD_DIM)

            def body(blk, carry, lanes=lanes, r=r):
                i0 = pl.multiple_of(blk * ATT_BLOCK, ATT_BLOCK)
                ws = pl.multiple_of(jnp.clip(i0 - ATT_HALF, 0, length - kw), ATT_HALF)
                q = q_ref[0, 0, pl.ds(i0, ATT_BLOCK), lanes]
                k = k_ref[0, 0, pl.ds(ws, kw), lanes]
                v = v_ref[0, 0, pl.ds(ws, kw), lanes]
                s = lax.dot_general(q, k, (((1,), (1,)), ((), ())), preferred_element_type=F32) * scale
                arel = jnp.abs(col_minus_row + (ws - i0))
                bias = -slope * (dil * arel).astype(F32)
                s = jnp.where(arel <= ATT_HALF, s + bias, NEG_BIG)
                m_blk = jnp.max(s, axis=-1, keepdims=True)
                if dil == 1:
                    rows = pl.ds(i0, ATT_BLOCK)
                else:
                    rows = pl.ds(i0 * dil + r, ATT_BLOCK, stride=dil)
                if first:
                    m_new = m_blk
                    p = jnp.exp(s - m_new)
                    l_new = jnp.sum(p, axis=-1, keepdims=True)
                    acc_new = jnp.dot(p.astype(BF16), v, preferred_element_type=F32)
                else:
                    m_old = m_ref[rows, :][:, :1]
                    m_new = jnp.maximum(m_old, m_blk)
                    alpha = jnp.exp(m_old - m_new)
                    p = jnp.exp(s - m_new)
                    l_new = alpha * l_ref[rows, :][:, :1] + jnp.sum(p, axis=-1, keepdims=True)
                    acc_new = alpha * acc_ref[rows, :] + jnp.dot(p.astype(BF16), v, preferred_element_type=F32)
                m_ref[rows, :] = jnp.broadcast_to(m_new, (ATT_BLOCK, HEAD_DIM))
                l_ref[rows, :] = jnp.broadcast_to(l_new, (ATT_BLOCK, HEAD_DIM))
                acc_ref[rows, :] = acc_new
                return carry

            lax.fori_loop(0, nblk, body, 0)

    run_group(2, True)
    run_group(1, False)
    run_group(0, False)
    o_ref[0] = (acc_ref[...] / l_ref[...]).astype(o_ref.dtype)


def _dilated_attention(proj):
    b, nh, s, hd = proj.shape
    args, specs = [], []
    for g, (_, dil) in enumerate(DIL_GROUPS):
        view = proj.reshape(b, nh, s // dil, dil * hd)
        for c in range(3):
            args.append(view)
            specs.append(pl.BlockSpec((1, 1, s // dil, dil * hd),
                                      lambda bi, hi, c=c, g=g: (bi, (c * N_GROUPS + g) * A_SLOTS + hi, 0, 0)))
    return pl.pallas_call(
        functools.partial(_dil_attn_kernel, seq=s),
        grid=(b, A_SLOTS),
        in_specs=specs,
        out_specs=pl.BlockSpec((1, s, hd), lambda bi, hi: (bi, 0, hi)),
        out_shape=jax.ShapeDtypeStruct((b, s, A_SLOTS * hd), BF16),
        scratch_shapes=[pltpu.VMEM((s, hd), F32), pltpu.VMEM((s, hd), F32), pltpu.VMEM((s, hd), F32)],
        compiler_params=_params(("parallel", "parallel")),
        name="dilated_attention",
    )(*args)


def _retention_kernel(dec_ref, q_ref, k_ref, v_ref, gate_ref, o_ref,
                      y_ref, st_ref, qdf_ref, kdf_ref, qdb_ref, kdb_ref, intra_ref, *, seq):
    h = pl.program_id(1)
    c = RET_CHUNK
    n = seq // c
    dk = RET_DK

    def log_gamma(e, shape):
        return jnp.log1p(-jnp.exp2(-jnp.full(shape, e, F32)))

    e_f = dec_ref[0, h]
    e_b = dec_ref[1, h]
    lg_f = log_gamma(e_f, (c, dk))
    lg_b = log_gamma(e_b, (c, dk))
    i_row = lax.broadcasted_iota(jnp.int32, (c, dk), 0).astype(F32)
    qdf_ref[...] = jnp.exp((i_row + 1.0) * lg_f)
    kdf_ref[...] = jnp.exp((c - 1.0 - i_row) * lg_f)
    qdb_ref[...] = jnp.exp((c - i_row) * lg_b)
    kdb_ref[...] = jnp.exp(i_row * lg_b)
    cd_f = jnp.exp(c * log_gamma(e_f, (1, dk)))
    cd_b = jnp.exp(c * log_gamma(e_b, (1, dk)))
    t = lax.broadcasted_iota(jnp.int32, (c, c), 0)
    s = lax.broadcasted_iota(jnp.int32, (c, c), 1)
    diff = (t - s).astype(F32)
    intra_ref[...] = jnp.where(t >= s, jnp.exp(jnp.where(t >= s, diff, 0.0) * log_gamma(e_f, (c, c))),
                               jnp.exp(jnp.where(t < s, -diff, 0.0) * log_gamma(e_b, (c, c))))

    def load(ci):
        rows = pl.ds(pl.multiple_of(ci * c, c), c)
        q = q_ref[0, 0, rows, :].astype(F32) * (dk ** -0.5)
        k = k_ref[0, 0, rows, :].astype(F32)
        v = v_ref[0, 0, rows, :]
        return rows, q, k, v

    def state_update(k_dec, v, cd):
        kv = lax.dot_general(k_dec.astype(BF16), v, (((0,), (0,)), ((), ())), preferred_element_type=F32)
        st_ref[...] = cd * st_ref[...] + kv

    st_ref[...] = jnp.zeros_like(st_ref)

    def fwd(ci, carry):
        rows, q, k, v = load(ci)
        sc = lax.dot_general(q.astype(BF16), k.astype(BF16), (((1,), (1,)), ((), ())),
                             preferred_element_type=F32) * intra_ref[...]
        y = jnp.dot(sc.astype(BF16), v, preferred_element_type=F32)
        y = y + jnp.dot((q * qdf_ref[...]).astype(BF16), st_ref[...].astype(BF16), preferred_element_type=F32)
        y_ref[rows, :] = y
        state_update(k * kdf_ref[...], v, cd_f)
        return carry

    lax.fori_loop(0, n, fwd, 0)

    st_ref[...] = jnp.zeros_like(st_ref)

    def bwd(step, carry):
        rows, q, k, v = load(n - 1 - step)
        y = jnp.dot((q * qdb_ref[...]).astype(BF16), st_ref[...].astype(BF16), preferred_element_type=F32)
        y_ref[rows, :] = y_ref[rows, :] + y
        state_update(k * kdb_ref[...], v, cd_b)
        return carry

    lax.fori_loop(0, n, bwd, 0)

    def finish(ci, carry):
        rows = pl.ds(pl.multiple_of(ci * c, c), c)
        y = y_ref[rows, :]
        y = y * lax.rsqrt(jnp.mean(y * y, axis=-1, keepdims=True) + EPS)
        gate = gate_ref[0, 0, rows, :].astype(F32)
        o_ref[0, rows, :] = (y * (gate * jax.nn.sigmoid(gate))).astype(o_ref.dtype)
        return carry

    lax.fori_loop(0, n, finish, 0)


def _retention(proj, decays):
    b, _, s, dk = proj.shape
    c = RET_CHUNK

    def head_spec(offset):
        return pl.BlockSpec((1, 1, s, dk), lambda bi, hi: (bi, offset + hi, 0, 0))

    return pl.pallas_call(
        functools.partial(_retention_kernel, seq=s),
        grid=(b, RET_HEADS),
        in_specs=[pl.BlockSpec(memory_space=pltpu.SMEM)] + [head_spec(RET_HEADS * i) for i in range(4)],
        out_specs=pl.BlockSpec((1, s, dk), lambda bi, hi: (bi, 0, hi)),
        out_shape=jax.ShapeDtypeStruct((b, s, RET_HEADS * dk), BF16),
        scratch_shapes=[pltpu.VMEM((s, dk), F32), pltpu.VMEM((dk, dk), F32)]
        + [pltpu.VMEM((c, dk), F32)] * 4 + [pltpu.VMEM((c, c), F32)],
        compiler_params=_params(("parallel", "parallel")),
        name="retention",
    )(decays, proj, proj, proj, proj)


def _cross_kernel(q_ref, kv_ref, o_ref, *, q_w):
    scale = HEAD_DIM ** -0.5
    per = q_w // HEAD_DIM
    for hh in range(CROSS_HEADS):
        q = q_ref[0, hh // per][:, (hh % per) * HEAD_DIM:(hh % per + 1) * HEAD_DIM]
        k = kv_ref[0, hh]
        v = kv_ref[0, CROSS_HEADS + hh]
        s = lax.dot_general(q, k, (((1,), (1,)), ((), ())), preferred_element_type=F32) * scale
        m = jnp.max(s, axis=-1, keepdims=True)
        p = jnp.exp(s - m)
        l = jnp.sum(p, axis=-1, keepdims=True)
        o = jnp.dot(p.astype(BF16), v, preferred_element_type=F32) / l
        o_ref[0, :, hh * HEAD_DIM:(hh + 1) * HEAD_DIM] = o.astype(o_ref.dtype)


def _cross_attention(proj, q_head0, kv, *, tm):
    b, _, s, q_w = proj.shape
    nq = CROSS_DIM // q_w
    return pl.pallas_call(
        functools.partial(_cross_kernel, q_w=q_w),
        grid=(b, s // tm),
        in_specs=[
            pl.BlockSpec((1, nq, tm, q_w), lambda bi, i: (bi, q_head0 // nq, i, 0)),
            pl.BlockSpec((1, 2 * CROSS_HEADS, N_MEM, HEAD_DIM), lambda bi, i: (bi, 0, 0, 0)),
        ],
        out_specs=pl.BlockSpec((1, tm, CROSS_DIM), lambda bi, i: (bi, i, 0)),
        out_shape=jax.ShapeDtypeStruct((b, s, CROSS_DIM), BF16),
        compiler_params=_params(("parallel", "parallel")),
        name="cross_attention",
    )(proj, kv)


def _out_proj_kernel(x_ref, mix_ref, cross_ref, w1_ref, w2_ref, o_ref):
    acc = jnp.dot(mix_ref[0], w1_ref[...], preferred_element_type=F32)
    acc = acc + jnp.dot(cross_ref[0], w2_ref[...], preferred_element_type=F32)
    o_ref[0] = x_ref[0] + acc


def _out_proj(x, mix, cross, w1, w2, *, tm, tn):
    b, s, d = x.shape
    km = mix.shape[-1]
    return pl.pallas_call(
        _out_proj_kernel,
        grid=(b, s // tm, d // tn),
        in_specs=[
            pl.BlockSpec((1, tm, tn), lambda bi, i, j: (bi, i, j)),
            pl.BlockSpec((1, tm, km), lambda bi, i, j: (bi, i, 0)),
            pl.BlockSpec((1, tm, CROSS_DIM), lambda bi, i, j: (bi, i, 0)),
            pl.BlockSpec((km, tn), lambda bi, i, j: (0, j)),
            pl.BlockSpec((CROSS_DIM, tn), lambda bi, i, j: (0, j)),
        ],
        out_specs=pl.BlockSpec((1, tm, tn), lambda bi, i, j: (bi, i, j)),
        out_shape=jax.ShapeDtypeStruct((b, s, d), F32),
        compiler_params=_params(("parallel", "parallel", "arbitrary")),
        name="out_proj",
    )(x, mix, cross, w1, w2)


def _ffn_kernel(x_ref, g_ref, gf_ref, wg_ref, wu_ref, wd_ref, o_ref, xn_ref, *, final_norm):
    f = pl.program_id(2)

    @pl.when(f == 0)
    def _():
        x = x_ref[0]
        xn_ref[...] = _rmsnorm_f32(x, g_ref[...]).astype(BF16)
        o_ref[0] = x

    xn = xn_ref[...]
    gate = jnp.dot(xn, wg_ref[...], preferred_element_type=F32)
    up = jnp.dot(xn, wu_ref[...], preferred_element_type=F32)
    act = (gate * jax.nn.sigmoid(gate)) * up
    o_ref[0] += jnp.dot(act.astype(BF16), wd_ref[...], preferred_element_type=F32)

    if final_norm:
        @pl.when(f == pl.num_programs(2) - 1)
        def _():
            o_ref[0] = _rmsnorm_f32(o_ref[0], gf_ref[...])


def _ffn(x, g, g_final, w_gate_up, w_down, *, final_norm, tm, tf):
    b, s, d = x.shape
    ff = w_down.shape[0]
    nf = ff // tf
    return pl.pallas_call(
        functools.partial(_ffn_kernel, final_norm=final_norm),
        grid=(b, s // tm, nf),
        in_specs=[
            pl.BlockSpec((1, tm, d), lambda bi, i, f: (bi, i, 0)),
            pl.BlockSpec((1, d), lambda bi, i, f: (0, 0)),
            pl.BlockSpec((1, d), lambda bi, i, f: (0, 0)),
            pl.BlockSpec((d, tf), lambda bi, i, f: (0, f)),
            pl.BlockSpec((d, tf), lambda bi, i, f: (0, nf + f)),
            pl.BlockSpec((tf, d), lambda bi, i, f: (f, 0)),
        ],
        out_specs=pl.BlockSpec((1, tm, d), lambda bi, i, f: (bi, i, 0)),
        out_shape=jax.ShapeDtypeStruct((b, s, d), F32),
        scratch_shapes=[pltpu.VMEM((tm, d), BF16)],
        compiler_params=_params(("parallel", "parallel", "arbitrary")),
        name="ffn",
    )(x, g.reshape(1, d), g_final.reshape(1, d), w_gate_up, w_gate_up, w_down)


def _trunk(x, mem, p):
    b, s, d = x.shape
    tm = 512
    for i in range(2):
        kv = _norm_proj(mem, p["norm_mem"][i], p["w_mem_kv"][i], head_w=HEAD_DIM, tm=N_MEM, tn=512)
        if i == 0:
            proj = _norm_proj(x, p["norm_mix"][i], p["a_w_in"], head_w=HEAD_DIM, tm=tm, tn=512)
            mix = _dilated_attention(proj)
            cross = _cross_attention(proj, 3 * A_HEADS, kv, tm=tm)
            w1, w2 = p["a_w_out1"], p["a_w_out2"]
        else:
            proj = _norm_proj(x, p["norm_mix"][i], p["b_w_in"], head_w=RET_DK, tm=tm, tn=512)
            mix = _retention(proj, p["b_decay"])
            cross = _cross_attention(proj, 4 * RET_HEADS, kv, tm=tm)
            w1, w2 = p["b_w_out1"], p["b_w_out2"]
        x = _out_proj(x, mix, cross, w1, w2, tm=tm, tn=1024)
        x = _ffn(x, p["norm_ffn"][i], p["norm_final"], p["w_gate_up"][i], p["w_down"][i],
                 final_norm=(i == 1), tm=tm, tf=512)
    return x


def kernel(x_prompt, x_sample, mem_prompt, mem_sample, norm_mix, norm_mem, w_mem_kv, a_w_in, a_w_out,
           b_w_in, b_w_out, b_decay_fwd, b_decay_bwd, norm_ffn, w_gate_up, w_down, norm_final):
    a_mix = A_SLOTS * HEAD_DIM
    b_mix = RET_HEADS * RET_DK
    p = {
        "norm_mix": norm_mix, "norm_mem": norm_mem, "norm_ffn": norm_ffn, "norm_final": norm_final,
        "w_mem_kv": w_mem_kv.astype(BF16),
        "a_w_in": a_w_in[0].astype(BF16),
        "a_w_out1": a_w_out[0, :a_mix].astype(BF16), "a_w_out2": a_w_out[0, a_mix:].astype(BF16),
        "b_w_in": b_w_in[0].astype(BF16),
        "b_w_out1": b_w_out[0, :b_mix].astype(BF16), "b_w_out2": b_w_out[0, b_mix:].astype(BF16),
        "b_decay": jnp.stack([b_decay_fwd[0], b_decay_bwd[0]]).astype(F32),
        "w_gate_up": w_gate_up.astype(BF16), "w_down": w_down.astype(BF16),
    }
    return _trunk(x_prompt, mem_prompt, p), _trunk(x_sample, mem_sample, p)
```

```python
import functools

import jax
import jax.numpy as jnp
from jax import lax
from jax.experimental import pallas as pl
from jax.experimental.pallas import tpu as pltpu

F32 = jnp.float32
BF16 = jnp.bfloat16

D_MODEL = 2048
HEAD_DIM = 128
DIL_GROUPS = ((128, 1), (512, 4), (2048, 16))
N_GROUPS = len(DIL_GROUPS)
A_SLOTS = D_MODEL // 256
A_HEADS = N_GROUPS * A_SLOTS
RET_HEADS = D_MODEL // 256
RET_DK = 256
RET_CHUNK = 128
CROSS_HEADS = 4
CROSS_DIM = CROSS_HEADS * HEAD_DIM
N_MEM = 256
D_FF = 5632
EPS = 1e-6
NEG_BIG = -1e30
LOG2_E = 1.4426950408889634

ATT_BLOCK = 128
ATT_HALF = 64
ATT_KEYS = 256
DIL_RATIO = 4

V7X_VMEM_LIMIT = 56 * 1024 * 1024


def _params(semantics):
    return pltpu.CompilerParams(dimension_semantics=semantics, vmem_limit_bytes=V7X_VMEM_LIMIT)


def _rmsnorm_f32(x, g):
    ms = jnp.mean(x * x, axis=-1, keepdims=True)
    return (x * lax.rsqrt(ms + EPS)) * g


def _norm_proj_kernel(x_ref, g_ref, w_ref, o_ref, xn_ref, *scratch, heads_per_step, head_w, dil):
    @pl.when(pl.program_id(2) == 0)
    def _():
        xn_ref[...] = _rmsnorm_f32(x_ref[0], g_ref[...]).astype(BF16)

    res = jnp.dot(xn_ref[...], w_ref[...], preferred_element_type=F32)
    if dil == 1:
        for h in range(heads_per_step):
            o_ref[0, h] = res[:, h * head_w:(h + 1) * head_w].astype(o_ref.dtype)
    else:
        stage_ref, = scratch
        rows = res.shape[0] // dil
        for h in range(heads_per_step):
            stage_ref[h] = res[:, h * head_w:(h + 1) * head_w]
        for h in range(heads_per_step):
            for r in range(dil):
                o_ref[0, h, r] = stage_ref[h, pl.ds(r, rows, stride=dil), :].astype(o_ref.dtype)


def _norm_proj(x, g, w, *, head_w, tm, tn, dil=1):
    b, s, d = x.shape
    n = w.shape[1]
    hps = tn // head_w
    kern = functools.partial(_norm_proj_kernel, heads_per_step=hps, head_w=head_w, dil=dil)
    scratch = [pltpu.VMEM((tm, d), BF16)]
    if dil == 1:
        out_spec = pl.BlockSpec((1, hps, tm, head_w), lambda bi, i, j: (bi, j, i, 0))
        out_shape = jax.ShapeDtypeStruct((b, n // head_w, s, head_w), BF16)
    else:
        out_spec = pl.BlockSpec((1, hps, dil, tm // dil, head_w), lambda bi, i, j: (bi, j, 0, i, 0))
        out_shape = jax.ShapeDtypeStruct((b, n // head_w, dil, s // dil, head_w), BF16)
        scratch.append(pltpu.VMEM((hps, tm, head_w), F32))
    return pl.pallas_call(
        kern,
        grid=(b, s // tm, n // tn),
        in_specs=[
            pl.BlockSpec((1, tm, d), lambda bi, i, j: (bi, i, 0)),
            pl.BlockSpec((1, d), lambda bi, i, j: (0, 0)),
            pl.BlockSpec((d, tn), lambda bi, i, j: (0, j)),
        ],
        out_specs=out_spec,
        out_shape=out_shape,
        scratch_shapes=scratch,
        compiler_params=_params(("parallel", "parallel", "arbitrary")),
        name="norm_proj",
    )(x, g.reshape(1, d), w)


def _dil_attn_kernel(q0, k0, v0, q1, k1, v1, q2, k2, v2, o_ref,
                     o1_ref, l1_ref, o0_ref, l0_ref, bias0_ref, bias1_ref, bias2_ref, *, seq, unroll):
    h = pl.program_id(1)
    scale = HEAD_DIM ** -0.5 * LOG2_E
    groups = ((q0, k0, v0, bias0_ref), (q1, k1, v1, bias1_ref), (q2, k2, v2, bias2_ref))

    def build_bias(g):
        bias_ref = groups[g][3]
        dil = DIL_GROUPS[g][1]
        kw = bias_ref.shape[-1]
        jf = (h + (g * A_SLOTS + 1)).astype(F32)
        slope = jnp.exp2(jnp.full((ATT_BLOCK, kw), -8.0 / A_HEADS, F32) * jf)
        row = lax.broadcasted_iota(jnp.int32, (ATT_BLOCK, kw), 0)
        col = lax.broadcasted_iota(jnp.int32, (ATT_BLOCK, kw), 1)
        for t, delta in enumerate((0, -ATT_HALF, ATT_BLOCK - kw)):
            arel = jnp.abs(col - row + delta)
            bias_ref[t] = jnp.where(arel <= ATT_HALF, (-slope * (dil * arel).astype(F32)) * LOG2_E, NEG_BIG)

    def attend(g, q, k, v, blk, nblk):
        bias_ref = groups[g][3]
        table = jnp.where(blk == 0, 0, jnp.where(blk == nblk - 1, 2, 1))
        s = lax.dot_general(q, k, (((1,), (1,)), ((), ())), preferred_element_type=F32) * scale
        s = s + bias_ref[table]
        m = jnp.max(s, axis=-1, keepdims=True)
        p = jnp.exp2(s - m)
        l = jnp.sum(p, axis=-1, keepdims=True)
        o = jnp.dot(p.astype(BF16), v, preferred_element_type=F32) / l
        lse = jnp.broadcast_to(m + jnp.log2(l), (ATT_BLOCK, HEAD_DIM))
        return o, lse

    def merge(o_old, l_old, o_new, l_new):
        mx = jnp.maximum(l_old, l_new)
        e_old = jnp.exp2(l_old - mx)
        e_new = jnp.exp2(l_new - mx)
        den = e_old + e_new
        return (e_old * o_old + e_new * o_new) / den, mx + jnp.log2(den)

    def run_group(g):
        q_ref, k_ref, v_ref, bias_ref = groups[g]
        dil = DIL_GROUPS[g][1]
        length = seq // dil
        kw = bias_ref.shape[-1]
        nblk = length // ATT_BLOCK
        shift = nblk.bit_length() - 1

        def body(it, carry):
            r = it >> shift
            blk = it & (nblk - 1)
            i0 = pl.multiple_of(blk * ATT_BLOCK, ATT_BLOCK)
            ws = pl.multiple_of(jnp.clip(i0 - ATT_HALF, 0, length - kw), ATT_HALF)
            if g == 0:
                q = q_ref[0, 0, pl.ds(i0, ATT_BLOCK), :]
                k = k_ref[0, 0, pl.ds(ws, kw), :]
                v = v_ref[0, 0, pl.ds(ws, kw), :]
            else:
                q = q_ref[0, 0, r, pl.ds(i0, ATT_BLOCK), :]
                k = k_ref[0, 0, r, pl.ds(ws, kw), :]
                v = v_ref[0, 0, r, pl.ds(ws, kw), :]
            o, lse = attend(g, q, k, v, blk, nblk)
            if g == 2:
                rows = pl.ds(i0 * DIL_RATIO + (r >> 2), ATT_BLOCK, stride=DIL_RATIO)
                o1_ref[r & 3, rows, :] = o
                l1_ref[r & 3, rows, :] = lse
            elif g == 1:
                o, lse = merge(o1_ref[r, pl.ds(i0, ATT_BLOCK), :], l1_ref[r, pl.ds(i0, ATT_BLOCK), :], o, lse)
                rows = pl.ds(i0 * DIL_RATIO + r, ATT_BLOCK, stride=DIL_RATIO)
                o0_ref[rows, :] = o
                l0_ref[rows, :] = lse
            else:
                rows = pl.ds(i0, ATT_BLOCK)
                o, _ = merge(o0_ref[rows, :], l0_ref[rows, :], o, lse)
                o_ref[0, rows, :] = o.astype(o_ref.dtype)
            return carry

        total = dil * nblk
        lax.fori_loop(0, total, body, 0, unroll=min(unroll, total))

    for g in range(N_GROUPS):
        build_bias(g)
    run_group(2)
    run_group(1)
    run_group(0)


def _dilated_attention(p0, p1, p2, *, unroll=8):
    b, _, s, hd = p0.shape
    args, specs = [], []
    for c in range(3):
        args.append(p0)
        specs.append(pl.BlockSpec((1, 1, s, hd), lambda bi, hi, c=c: (bi, c * A_SLOTS + hi, 0, 0)))
    for pg in (p1, p2):
        dil, length = pg.shape[2], pg.shape[3]
        for c in range(3):
            args.append(pg)
            specs.append(pl.BlockSpec((1, 1, dil, length, hd), lambda bi, hi, c=c: (bi, c * A_SLOTS + hi, 0, 0, 0)))
    l1 = s // DIL_GROUPS[1][1]
    bias = [pltpu.VMEM((3, ATT_BLOCK, min(ATT_KEYS, s // dil)), F32) for _, dil in DIL_GROUPS]
    return pl.pallas_call(
        functools.partial(_dil_attn_kernel, seq=s, unroll=unroll),
        grid=(b, A_SLOTS),
        in_specs=specs,
        out_specs=pl.BlockSpec((1, s, hd), lambda bi, hi: (bi, 0, hi)),
        out_shape=jax.ShapeDtypeStruct((b, s, A_SLOTS * hd), BF16),
        scratch_shapes=[pltpu.VMEM((DIL_RATIO, l1, hd), F32), pltpu.VMEM((DIL_RATIO, l1, hd), F32),
                        pltpu.VMEM((s, hd), F32), pltpu.VMEM((s, hd), F32)] + bias,
        compiler_params=_params(("parallel", "parallel")),
        name="dilated_attention",
    )(*args)


def _retention_kernel(dec_ref, q_ref, k_ref, v_ref, gate_ref, o_ref,
                      yf_ref, yb_ref, sf_ref, sb_ref, qdf_ref, kdf_ref, qdb_ref, kdb_ref, intra_ref,
                      *, seq, unroll):
    h = pl.program_id(1)
    c = RET_CHUNK
    n = seq // c
    dk = RET_DK

    def log_gamma(e, shape):
        return jnp.log1p(-jnp.exp2(-jnp.full(shape, e, F32)))

    e_f = dec_ref[0, h]
    e_b = dec_ref[1, h]
    lg_f = log_gamma(e_f, (c, dk))
    lg_b = log_gamma(e_b, (c, dk))
    i_row = lax.broadcasted_iota(jnp.int32, (c, dk), 0).astype(F32)
    qdf_ref[...] = jnp.exp((i_row + 1.0) * lg_f)
    kdf_ref[...] = jnp.exp((c - 1.0 - i_row) * lg_f)
    qdb_ref[...] = jnp.exp((c - i_row) * lg_b)
    kdb_ref[...] = jnp.exp(i_row * lg_b)
    cd_f = jnp.exp(c * log_gamma(e_f, (1, dk)))
    cd_b = jnp.exp(c * log_gamma(e_b, (1, dk)))
    t = lax.broadcasted_iota(jnp.int32, (c, c), 0)
    s = lax.broadcasted_iota(jnp.int32, (c, c), 1)
    diff = (t - s).astype(F32)
    intra_ref[...] = jnp.where(t >= s, jnp.exp(jnp.where(t >= s, diff, 0.0) * log_gamma(e_f, (c, c))),
                               jnp.exp(jnp.where(t < s, -diff, 0.0) * log_gamma(e_b, (c, c))))

    def load(ci):
        rows = pl.ds(pl.multiple_of(ci * c, c), c)
        q = q_ref[0, 0, rows, :].astype(F32) * (dk ** -0.5)
        k = k_ref[0, 0, rows, :].astype(F32)
        v = v_ref[0, 0, rows, :]
        return rows, q, k, v

    def inter(q_dec, k_dec, v, st_ref, cd):
        y = jnp.dot(q_dec.astype(BF16), st_ref[...].astype(BF16), preferred_element_type=F32)
        kv = lax.dot_general(k_dec.astype(BF16), v, (((0,), (0,)), ((), ())), preferred_element_type=F32)
        st_ref[...] = cd * st_ref[...] + kv
        return y

    sf_ref[...] = jnp.zeros_like(sf_ref)
    sb_ref[...] = jnp.zeros_like(sb_ref)

    def step(ci, carry):
        rows, q, k, v = load(ci)
        sc = lax.dot_general(q.astype(BF16), k.astype(BF16), (((1,), (1,)), ((), ())),
                             preferred_element_type=F32) * intra_ref[...]
        y = jnp.dot(sc.astype(BF16), v, preferred_element_type=F32)
        yf_ref[rows, :] = y + inter(q * qdf_ref[...], k * kdf_ref[...], v, sf_ref, cd_f)
        rows, q, k, v = load(n - 1 - ci)
        yb_ref[rows, :] = inter(q * qdb_ref[...], k * kdb_ref[...], v, sb_ref, cd_b)
        return carry

    lax.fori_loop(0, n, step, 0, unroll=unroll)

    def finish(ci, carry):
        rows = pl.ds(pl.multiple_of(ci * c, c), c)
        y = yf_ref[rows, :] + yb_ref[rows, :]
        y = y * lax.rsqrt(jnp.mean(y * y, axis=-1, keepdims=True) + EPS)
        gate = gate_ref[0, 0, rows, :].astype(F32)
        o_ref[0, rows, :] = (y * (gate * jax.nn.sigmoid(gate))).astype(o_ref.dtype)
        return carry

    lax.fori_loop(0, n, finish, 0, unroll=unroll)


def _retention(proj, decays, *, unroll=4):
    b, _, s, dk = proj.shape
    c = RET_CHUNK

    def head_spec(offset):
        return pl.BlockSpec((1, 1, s, dk), lambda bi, hi: (bi, offset + hi, 0, 0))

    return pl.pallas_call(
        functools.partial(_retention_kernel, seq=s, unroll=unroll),
        grid=(b, RET_HEADS),
        in_specs=[pl.BlockSpec(memory_space=pltpu.SMEM)] + [head_spec(RET_HEADS * i) for i in range(4)],
        out_specs=pl.BlockSpec((1, s, dk), lambda bi, hi: (bi, 0, hi)),
        out_shape=jax.ShapeDtypeStruct((b, s, RET_HEADS * dk), BF16),
        scratch_shapes=[pltpu.VMEM((s, dk), F32)] * 2 + [pltpu.VMEM((dk, dk), F32)] * 2
        + [pltpu.VMEM((c, dk), F32)] * 4 + [pltpu.VMEM((c, c), F32)],
        compiler_params=_params(("parallel", "parallel")),
        name="retention",
    )(decays, proj, proj, proj, proj)


def _cross_kernel(q_ref, kv_ref, o_ref, *, q_w):
    scale = HEAD_DIM ** -0.5
    per = q_w // HEAD_DIM
    for hh in range(CROSS_HEADS):
        q = q_ref[0, hh // per][:, (hh % per) * HEAD_DIM:(hh % per + 1) * HEAD_DIM]
        k = kv_ref[0, hh]
        v = kv_ref[0, CROSS_HEADS + hh]
        s = lax.dot_general(q, k, (((1,), (1,)), ((), ())), preferred_element_type=F32) * scale
        m = jnp.max(s, axis=-1, keepdims=True)
        p = jnp.exp(s - m)
        l = jnp.sum(p, axis=-1, keepdims=True)
        o = jnp.dot(p.astype(BF16), v, preferred_element_type=F32) / l
        o_ref[0, :, hh * HEAD_DIM:(hh + 1) * HEAD_DIM] = o.astype(o_ref.dtype)


def _cross_attention(proj, q_head0, kv, *, tm):
    b, _, s, q_w = proj.shape
    nq = CROSS_DIM // q_w
    return pl.pallas_call(
        functools.partial(_cross_kernel, q_w=q_w),
        grid=(b, s // tm),
        in_specs=[
            pl.BlockSpec((1, nq, tm, q_w), lambda bi, i: (bi, q_head0 // nq, i, 0)),
            pl.BlockSpec((1, 2 * CROSS_HEADS, N_MEM, HEAD_DIM), lambda bi, i: (bi, 0, 0, 0)),
        ],
        out_specs=pl.BlockSpec((1, tm, CROSS_DIM), lambda bi, i: (bi, i, 0)),
        out_shape=jax.ShapeDtypeStruct((b, s, CROSS_DIM), BF16),
        compiler_params=_params(("parallel", "parallel")),
        name="cross_attention",
    )(proj, kv)


def _out_proj_kernel(x_ref, mix_ref, cross_ref, w1_ref, w2_ref, o_ref):
    acc = jnp.dot(mix_ref[0], w1_ref[...], preferred_element_type=F32)
    acc = acc + jnp.dot(cross_ref[0], w2_ref[...], preferred_element_type=F32)
    o_ref[0] = x_ref[0] + acc


def _out_proj(x, mix, cross, w1, w2, *, tm, tn):
    b, s, d = x.shape
    km = mix.shape[-1]
    return pl.pallas_call(
        _out_proj_kernel,
        grid=(b, s // tm, d // tn),
        in_specs=[
            pl.BlockSpec((1, tm, tn), lambda bi, i, j: (bi, i, j)),
            pl.BlockSpec((1, tm, km), lambda bi, i, j: (bi, i, 0)),
            pl.BlockSpec((1, tm, CROSS_DIM), lambda bi, i, j: (bi, i, 0)),
            pl.BlockSpec((km, tn), lambda bi, i, j: (0, j)),
            pl.BlockSpec((CROSS_DIM, tn), lambda bi, i, j: (0, j)),
        ],
        out_specs=pl.BlockSpec((1, tm, tn), lambda bi, i, j: (bi, i, j)),
        out_shape=jax.ShapeDtypeStruct((b, s, d), F32),
        compiler_params=_params(("parallel", "parallel", "arbitrary")),
        name="out_proj",
    )(x, mix, cross, w1, w2)


def _ffn_kernel(x_ref, g_ref, gf_ref, wg_ref, wu_ref, wd_ref, o_ref, xn_ref, *, final_norm):
    f = pl.program_id(2)

    @pl.when(f == 0)
    def _():
        x = x_ref[0]
        xn_ref[...] = _rmsnorm_f32(x, g_ref[...]).astype(BF16)
        o_ref[0] = x

    xn = xn_ref[...]
    gate = jnp.dot(xn, wg_ref[...], preferred_element_type=F32)
    up = jnp.dot(xn, wu_ref[...], preferred_element_type=F32)
    act = (gate * jax.nn.sigmoid(gate)) * up
    o_ref[0] += jnp.dot(act.astype(BF16), wd_ref[...], preferred_element_type=F32)

    if final_norm:
        @pl.when(f == pl.num_programs(2) - 1)
        def _():
            o_ref[0] = _rmsnorm_f32(o_ref[0], gf_ref[...])


def _ffn(x, g, g_final, w_gate_up, w_down, *, final_norm, tm, tf):
    b, s, d = x.shape
    ff = w_down.shape[0]
    nf = ff // tf
    return pl.pallas_call(
        functools.partial(_ffn_kernel, final_norm=final_norm),
        grid=(b, s // tm, nf),
        in_specs=[
            pl.BlockSpec((1, tm, d), lambda bi, i, f: (bi, i, 0)),
            pl.BlockSpec((1, d), lambda bi, i, f: (0, 0)),
            pl.BlockSpec((1, d), lambda bi, i, f: (0, 0)),
            pl.BlockSpec((d, tf), lambda bi, i, f: (0, f)),
            pl.BlockSpec((d, tf), lambda bi, i, f: (0, nf + f)),
            pl.BlockSpec((tf, d), lambda bi, i, f: (f, 0)),
        ],
        out_specs=pl.BlockSpec((1, tm, d), lambda bi, i, f: (bi, i, 0)),
        out_shape=jax.ShapeDtypeStruct((b, s, d), F32),
        scratch_shapes=[pltpu.VMEM((tm, d), BF16)],
        compiler_params=_params(("parallel", "parallel", "arbitrary")),
        name="ffn",
    )(x, g.reshape(1, d), g_final.reshape(1, d), w_gate_up, w_gate_up, w_down)


TM_PROJ = 1024
TM_OUT = 512
TM_FFN = 1024
TN = 512


def _trunk(x, mem, p):
    for i in range(2):
        kv = _norm_proj(mem, p["norm_mem"][i], p["w_mem_kv"][i], head_w=HEAD_DIM, tm=N_MEM, tn=TN)
        if i == 0:
            projs = [_norm_proj(x, p["norm_mix"][i], p["a_w_in"][g], head_w=HEAD_DIM, tm=TM_PROJ, tn=TN, dil=dil)
                     for g, (_, dil) in enumerate(DIL_GROUPS)]
            mix = _dilated_attention(*projs)
            cross = _cross_attention(projs[0], 3 * A_SLOTS, kv, tm=TM_OUT)
            w1, w2 = p["a_w_out1"], p["a_w_out2"]
        else:
            proj = _norm_proj(x, p["norm_mix"][i], p["b_w_in"], head_w=RET_DK, tm=TM_PROJ, tn=TN)
            mix = _retention(proj, p["b_decay"])
            cross = _cross_attention(proj, 4 * RET_HEADS, kv, tm=TM_OUT)
            w1, w2 = p["b_w_out1"], p["b_w_out2"]
        x = _out_proj(x, mix, cross, w1, w2, tm=TM_OUT, tn=1024)
        x = _ffn(x, p["norm_ffn"][i], p["norm_final"], p["w_gate_up"][i], p["w_down"][i],
                 final_norm=(i == 1), tm=TM_FFN, tf=TN)
    return x


def _group_columns(w_in):
    gw = A_SLOTS * HEAD_DIM
    out = []
    for g in range(N_GROUPS):
        cols = [w_in[:, (c * N_GROUPS + g) * gw:(c * N_GROUPS + g + 1) * gw] for c in range(3)]
        if g == 0:
            cols.append(w_in[:, 3 * N_GROUPS * gw:])
        out.append(jnp.concatenate(cols, axis=1).astype(BF16))
    return out


def kernel(x_prompt, x_sample, mem_prompt, mem_sample, norm_mix, norm_mem, w_mem_kv, a_w_in, a_w_out,
           b_w_in, b_w_out, b_decay_fwd, b_decay_bwd, norm_ffn, w_gate_up, w_down, norm_final):
    a_mix = A_SLOTS * HEAD_DIM
    b_mix = RET_HEADS * RET_DK
    p = {
        "norm_mix": norm_mix, "norm_mem": norm_mem, "norm_ffn": norm_ffn, "norm_final": norm_final,
        "w_mem_kv": w_mem_kv.astype(BF16),
        "a_w_in": _group_columns(a_w_in[0]),
        "a_w_out1": a_w_out[0, :a_mix].astype(BF16), "a_w_out2": a_w_out[0, a_mix:].astype(BF16),
        "b_w_in": b_w_in[0].astype(BF16),
        "b_w_out1": b_w_out[0, :b_mix].astype(BF16), "b_w_out2": b_w_out[0, b_mix:].astype(BF16),
        "b_decay": jnp.stack([b_decay_fwd[0], b_decay_bwd[0]]).astype(F32),
        "w_gate_up": w_gate_up.astype(BF16), "w_down": w_down.astype(BF16),
    }
    return _trunk(x_prompt, mem_prompt, p), _trunk(x_sample, mem_sample, p)
```

```python
import functools

import jax
import jax.numpy as jnp
from jax import lax
from jax.experimental import pallas as pl
from jax.experimental.pallas import tpu as pltpu

F32 = jnp.float32
BF16 = jnp.bfloat16

D_MODEL = 2048
HEAD_DIM = 128
DIL_GROUPS = ((128, 1), (512, 4), (2048, 16))
N_GROUPS = len(DIL_GROUPS)
A_SLOTS = D_MODEL // 256
A_HEADS = N_GROUPS * A_SLOTS
RET_HEADS = D_MODEL // 256
RET_DK = 256
RET_CHUNK = 128
CROSS_HEADS = 4
CROSS_DIM = CROSS_HEADS * HEAD_DIM
N_MEM = 256
D_FF = 5632
EPS = 1e-6
NEG_BIG = -1e30
LOG2_E = 1.4426950408889634

ATT_BLOCK = 128
ATT_HALF = 64
ATT_KEYS = 256
DIL_RATIO = 4

V7X_VMEM_LIMIT = 56 * 1024 * 1024
MXU_COLS = 256


def _params(semantics):
    return pltpu.CompilerParams(dimension_semantics=semantics, vmem_limit_bytes=V7X_VMEM_LIMIT)


def _rmsnorm_f32(x, g):
    ms = jnp.mean(x * x, axis=-1, keepdims=True)
    return (x * lax.rsqrt(ms + EPS)) * g


def _norm_proj_kernel(x_ref, g_ref, w_ref, *rest, segments, head_w, half_w):
    n_seg = len(segments)
    o_refs, xn_ref, stage_ref = rest[:n_seg], rest[n_seg], rest[n_seg + 1]
    j = pl.program_id(2)
    tn = w_ref.shape[-1]
    per_half = half_w // head_w

    @pl.when(j == 0)
    def _():
        xn_ref[...] = _rmsnorm_f32(x_ref[0], g_ref[...]).astype(BF16)

    def project(half):
        return jnp.dot(xn_ref[...], w_ref[0, :, half * half_w:(half + 1) * half_w], preferred_element_type=F32)

    start = 0
    for o_ref, (steps, dil) in zip(o_refs, segments):
        def segment(o_ref=o_ref, dil=dil):
            for half in range(tn // half_w):
                res = project(half)
                for hh in range(per_half):
                    h = half * per_half + hh
                    cols = res[:, hh * head_w:(hh + 1) * head_w]
                    if dil == 1:
                        o_ref[0, h] = cols.astype(o_ref.dtype)
                    else:
                        stage_ref[h] = cols
                if dil > 1:
                    rows = res.shape[0] // dil
                    for hh in range(per_half):
                        h = half * per_half + hh
                        for r in range(dil):
                            o_ref[0, h, r] = stage_ref[h, pl.ds(r, rows, stride=dil), :].astype(o_ref.dtype)

        if n_seg == 1:
            segment()
        else:
            pl.when((j >= start) & (j < start + steps))(segment)
        start += steps


def _norm_proj(x, g, w_tiles, *, head_w, tm, segments=None):
    b, s, d = x.shape
    nt, _, tn = w_tiles.shape
    hps = tn // head_w
    segments = segments or ((nt, 1),)
    out_specs, out_shapes, start = [], [], 0
    for steps, dil in segments:
        def held(j, start=start, steps=steps):
            return jnp.clip(j - start, 0, steps - 1)
        if dil == 1:
            out_specs.append(pl.BlockSpec((1, hps, tm, head_w), lambda bi, i, j, held=held: (bi, held(j), i, 0)))
            out_shapes.append(jax.ShapeDtypeStruct((b, steps * hps, s, head_w), BF16))
        else:
            out_specs.append(pl.BlockSpec((1, hps, dil, tm // dil, head_w),
                                          lambda bi, i, j, held=held: (bi, held(j), 0, i, 0)))
            out_shapes.append(jax.ShapeDtypeStruct((b, steps * hps, dil, s // dil, head_w), BF16))
        start += steps
    kern = functools.partial(_norm_proj_kernel, segments=tuple(segments), head_w=head_w,
                             half_w=max(head_w, MXU_COLS))
    outs = pl.pallas_call(
        kern,
        grid=(b, s // tm, nt),
        in_specs=[
            pl.BlockSpec((1, tm, d), lambda bi, i, j: (bi, i, 0)),
            pl.BlockSpec((1, d), lambda bi, i, j: (0, 0)),
            pl.BlockSpec((1, d, tn), lambda bi, i, j: (j, 0, 0)),
        ],
        out_specs=out_specs,
        out_shape=out_shapes,
        scratch_shapes=[pltpu.VMEM((tm, d), BF16), pltpu.VMEM((hps, tm, head_w), F32)],
        compiler_params=_params(("parallel", "parallel", "arbitrary")),
        name="norm_proj",
    )(x, g.reshape(1, d), w_tiles)
    return outs if len(outs) > 1 else outs[0]


def _dil_attn_kernel(q0, k0, v0, q1, k1, v1, q2, k2, v2, o_ref,
                     o1_ref, l1_ref, o0_ref, l0_ref, bias0_ref, bias1_ref, bias2_ref, *, seq, unroll):
    h = pl.program_id(1)
    scale = HEAD_DIM ** -0.5 * LOG2_E
    groups = ((q0, k0, v0, bias0_ref), (q1, k1, v1, bias1_ref), (q2, k2, v2, bias2_ref))

    def build_bias(g):
        bias_ref = groups[g][3]
        dil = DIL_GROUPS[g][1]
        kw = bias_ref.shape[-1]
        jf = (h + (g * A_SLOTS + 1)).astype(F32)
        slope = jnp.exp2(jnp.full((ATT_BLOCK, kw), -8.0 / A_HEADS, F32) * jf)
        row = lax.broadcasted_iota(jnp.int32, (ATT_BLOCK, kw), 0)
        col = lax.broadcasted_iota(jnp.int32, (ATT_BLOCK, kw), 1)
        for t, delta in enumerate((0, -ATT_HALF, ATT_BLOCK - kw)):
            arel = jnp.abs(col - row + delta)
            bias_ref[t] = jnp.where(arel <= ATT_HALF, (-slope * (dil * arel).astype(F32)) * LOG2_E, NEG_BIG)

    def attend(g, q, k, v, blk, nblk):
        bias_ref = groups[g][3]
        table = jnp.where(blk == 0, 0, jnp.where(blk == nblk - 1, 2, 1))
        s = lax.dot_general(q, k, (((1,), (1,)), ((), ())), preferred_element_type=F32) * scale
        s = s + bias_ref[table]
        m = jnp.max(s, axis=-1, keepdims=True)
        p = jnp.exp2(s - m)
        l = jnp.sum(p, axis=-1, keepdims=True)
        o = jnp.dot(p.astype(BF16), v, preferred_element_type=F32) / l
        lse = jnp.broadcast_to(m + jnp.log2(l), (ATT_BLOCK, HEAD_DIM))
        return o, lse

    def merge(o_old, l_old, o_new, l_new):
        mx = jnp.maximum(l_old, l_new)
        e_old = jnp.exp2(l_old - mx)
        e_new = jnp.exp2(l_new - mx)
        den = e_old + e_new
        return (e_old * o_old + e_new * o_new) / den, mx + jnp.log2(den)

    def run_group(g):
        q_ref, k_ref, v_ref, bias_ref = groups[g]
        dil = DIL_GROUPS[g][1]
        length = seq // dil
        kw = bias_ref.shape[-1]
        nblk = length // ATT_BLOCK
        shift = nblk.bit_length() - 1

        def body(it, carry):
            r = it >> shift
            blk = it & (nblk - 1)
            i0 = pl.multiple_of(blk * ATT_BLOCK, ATT_BLOCK)
            ws = pl.multiple_of(jnp.clip(i0 - ATT_HALF, 0, length - kw), ATT_HALF)
            if g == 0:
                q = q_ref[0, 0, pl.ds(i0, ATT_BLOCK), :]
                k = k_ref[0, 0, pl.ds(ws, kw), :]
                v = v_ref[0, 0, pl.ds(ws, kw), :]
            else:
                q = q_ref[0, 0, r, pl.ds(i0, ATT_BLOCK), :]
                k = k_ref[0, 0, r, pl.ds(ws, kw), :]
                v = v_ref[0, 0, r, pl.ds(ws, kw), :]
            o, lse = attend(g, q, k, v, blk, nblk)
            if g == 2:
                rows = pl.ds(i0 * DIL_RATIO + (r >> 2), ATT_BLOCK, stride=DIL_RATIO)
                o1_ref[r & 3, rows, :] = o
                l1_ref[r & 3, rows, :] = lse
            elif g == 1:
                o, lse = merge(o1_ref[r, pl.ds(i0, ATT_BLOCK), :], l1_ref[r, pl.ds(i0, ATT_BLOCK), :], o, lse)
                rows = pl.ds(i0 * DIL_RATIO + r, ATT_BLOCK, stride=DIL_RATIO)
                o0_ref[rows, :] = o
                l0_ref[rows, :] = lse
            else:
                rows = pl.ds(i0, ATT_BLOCK)
                o, _ = merge(o0_ref[rows, :], l0_ref[rows, :], o, lse)
                o_ref[0, rows, :] = o.astype(o_ref.dtype)
            return carry

        total = dil * nblk
        lax.fori_loop(0, total, body, 0, unroll=min(unroll, total))

    for g in range(N_GROUPS):
        build_bias(g)
    run_group(2)
    run_group(1)
    run_group(0)


def _dilated_attention(p0, p1, p2, *, unroll=8):
    b, _, s, hd = p0.shape
    args, specs = [], []
    for c in range(3):
        args.append(p0)
        specs.append(pl.BlockSpec((1, 1, s, hd), lambda bi, hi, c=c: (bi, c * A_SLOTS + hi, 0, 0)))
    for pg in (p1, p2):
        dil, length = pg.shape[2], pg.shape[3]
        for c in range(3):
            args.append(pg)
            specs.append(pl.BlockSpec((1, 1, dil, length, hd), lambda bi, hi, c=c: (bi, c * A_SLOTS + hi, 0, 0, 0)))
    l1 = s // DIL_GROUPS[1][1]
    bias = [pltpu.VMEM((3, ATT_BLOCK, min(ATT_KEYS, s // dil)), F32) for _, dil in DIL_GROUPS]
    return pl.pallas_call(
        functools.partial(_dil_attn_kernel, seq=s, unroll=unroll),
        grid=(b, A_SLOTS),
        in_specs=specs,
        out_specs=pl.BlockSpec((1, s, hd), lambda bi, hi: (bi, 0, hi)),
        out_shape=jax.ShapeDtypeStruct((b, s, A_SLOTS * hd), BF16),
        scratch_shapes=[pltpu.VMEM((DIL_RATIO, l1, hd), F32), pltpu.VMEM((DIL_RATIO, l1, hd), F32),
                        pltpu.VMEM((s, hd), F32), pltpu.VMEM((s, hd), F32)] + bias,
        compiler_params=_params(("parallel", "parallel")),
        name="dilated_attention",
    )(*args)


def _retention_kernel(dec_ref, q_ref, k_ref, v_ref, gate_ref, o_ref,
                      yf_ref, yb_ref, sf_ref, sb_ref, qdf_ref, kdf_ref, qdb_ref, kdb_ref, intra_ref,
                      *, seq, unroll):
    h = pl.program_id(1)
    c = RET_CHUNK
    n = seq // c
    dk = RET_DK

    def log_gamma(e, shape):
        return jnp.log1p(-jnp.exp2(-jnp.full(shape, e, F32)))

    e_f = dec_ref[0, h]
    e_b = dec_ref[1, h]
    lg_f = log_gamma(e_f, (c, dk))
    lg_b = log_gamma(e_b, (c, dk))
    i_row = lax.broadcasted_iota(jnp.int32, (c, dk), 0).astype(F32)
    qdf_ref[...] = jnp.exp((i_row + 1.0) * lg_f)
    kdf_ref[...] = jnp.exp((c - 1.0 - i_row) * lg_f)
    qdb_ref[...] = jnp.exp((c - i_row) * lg_b)
    kdb_ref[...] = jnp.exp(i_row * lg_b)
    cd_f = jnp.exp(c * log_gamma(e_f, (1, dk)))
    cd_b = jnp.exp(c * log_gamma(e_b, (1, dk)))
    t = lax.broadcasted_iota(jnp.int32, (c, c), 0)
    s = lax.broadcasted_iota(jnp.int32, (c, c), 1)
    diff = (t - s).astype(F32)
    intra_ref[...] = jnp.where(t >= s, jnp.exp(jnp.where(t >= s, diff, 0.0) * log_gamma(e_f, (c, c))),
                               jnp.exp(jnp.where(t < s, -diff, 0.0) * log_gamma(e_b, (c, c))))

    def load(ci):
        rows = pl.ds(pl.multiple_of(ci * c, c), c)
        q = q_ref[0, 0, rows, :].astype(F32) * (dk ** -0.5)
        k = k_ref[0, 0, rows, :].astype(F32)
        v = v_ref[0, 0, rows, :]
        return rows, q, k, v

    def inter(q_dec, k_dec, v, st_ref, cd):
        y = jnp.dot(q_dec.astype(BF16), st_ref[...].astype(BF16), preferred_element_type=F32)
        kv = lax.dot_general(k_dec.astype(BF16), v, (((0,), (0,)), ((), ())), preferred_element_type=F32)
        st_ref[...] = cd * st_ref[...] + kv
        return y

    sf_ref[...] = jnp.zeros_like(sf_ref)
    sb_ref[...] = jnp.zeros_like(sb_ref)

    def step(ci, carry):
        rows, q, k, v = load(ci)
        sc = lax.dot_general(q.astype(BF16), k.astype(BF16), (((1,), (1,)), ((), ())),
                             preferred_element_type=F32) * intra_ref[...]
        y = jnp.dot(sc.astype(BF16), v, preferred_element_type=F32)
        yf_ref[rows, :] = y + inter(q * qdf_ref[...], k * kdf_ref[...], v, sf_ref, cd_f)
        rows, q, k, v = load(n - 1 - ci)
        yb_ref[rows, :] = inter(q * qdb_ref[...], k * kdb_ref[...], v, sb_ref, cd_b)
        return carry

    lax.fori_loop(0, n, step, 0, unroll=unroll)

    def finish(ci, carry):
        rows = pl.ds(pl.multiple_of(ci * c, c), c)
        y = yf_ref[rows, :] + yb_ref[rows, :]
        y = y * lax.rsqrt(jnp.mean(y * y, axis=-1, keepdims=True) + EPS)
        gate = gate_ref[0, 0, rows, :].astype(F32)
        o_ref[0, rows, :] = (y * (gate * jax.nn.sigmoid(gate))).astype(o_ref.dtype)
        return carry

    lax.fori_loop(0, n, finish, 0, unroll=unroll)


def _retention(proj, decays, *, unroll=4):
    b, _, s, dk = proj.shape
    c = RET_CHUNK

    def head_spec(offset):
        return pl.BlockSpec((1, 1, s, dk), lambda bi, hi: (bi, offset + hi, 0, 0))

    return pl.pallas_call(
        functools.partial(_retention_kernel, seq=s, unroll=unroll),
        grid=(b, RET_HEADS),
        in_specs=[pl.BlockSpec(memory_space=pltpu.SMEM)] + [head_spec(RET_HEADS * i) for i in range(4)],
        out_specs=pl.BlockSpec((1, s, dk), lambda bi, hi: (bi, 0, hi)),
        out_shape=jax.ShapeDtypeStruct((b, s, RET_HEADS * dk), BF16),
        scratch_shapes=[pltpu.VMEM((s, dk), F32)] * 2 + [pltpu.VMEM((dk, dk), F32)] * 2
        + [pltpu.VMEM((c, dk), F32)] * 4 + [pltpu.VMEM((c, c), F32)],
        compiler_params=_params(("parallel", "parallel")),
        name="retention",
    )(decays, proj, proj, proj, proj)


def _cross_kernel(q_ref, kv_ref, o_ref, *, q_w):
    scale = HEAD_DIM ** -0.5
    per = q_w // HEAD_DIM
    for hh in range(CROSS_HEADS):
        q = q_ref[0, hh // per][:, (hh % per) * HEAD_DIM:(hh % per + 1) * HEAD_DIM]
        k = kv_ref[0, hh]
        v = kv_ref[0, CROSS_HEADS + hh]
        s = lax.dot_general(q, k, (((1,), (1,)), ((), ())), preferred_element_type=F32) * scale
        m = jnp.max(s, axis=-1, keepdims=True)
        p = jnp.exp(s - m)
        l = jnp.sum(p, axis=-1, keepdims=True)
        o = jnp.dot(p.astype(BF16), v, preferred_element_type=F32) / l
        o_ref[0, :, hh * HEAD_DIM:(hh + 1) * HEAD_DIM] = o.astype(o_ref.dtype)


def _cross_attention(proj, q_head0, kv, *, tm):
    b, _, s, q_w = proj.shape
    nq = CROSS_DIM // q_w
    return pl.pallas_call(
        functools.partial(_cross_kernel, q_w=q_w),
        grid=(b, s // tm),
        in_specs=[
            pl.BlockSpec((1, nq, tm, q_w), lambda bi, i: (bi, q_head0 // nq, i, 0)),
            pl.BlockSpec((1, 2 * CROSS_HEADS, N_MEM, HEAD_DIM), lambda bi, i: (bi, 0, 0, 0)),
        ],
        out_specs=pl.BlockSpec((1, tm, CROSS_DIM), lambda bi, i: (bi, i, 0)),
        out_shape=jax.ShapeDtypeStruct((b, s, CROSS_DIM), BF16),
        compiler_params=_params(("parallel", "parallel")),
        name="cross_attention",
    )(proj, kv)


def _out_proj_kernel(x_ref, mix_ref, cross_ref, w1_ref, w2_ref, o_ref):
    acc = jnp.dot(mix_ref[0], w1_ref[...], preferred_element_type=F32)
    acc = acc + jnp.dot(cross_ref[0], w2_ref[...], preferred_element_type=F32)
    o_ref[0] = x_ref[0] + acc


def _out_proj(x, mix, cross, w1, w2, *, tm):
    b, s, d = x.shape
    km = mix.shape[-1]
    return pl.pallas_call(
        _out_proj_kernel,
        grid=(b, s // tm),
        in_specs=[
            pl.BlockSpec((1, tm, d), lambda bi, i: (bi, i, 0)),
            pl.BlockSpec((1, tm, km), lambda bi, i: (bi, i, 0)),
            pl.BlockSpec((1, tm, CROSS_DIM), lambda bi, i: (bi, i, 0)),
            pl.BlockSpec((km, d), lambda bi, i: (0, 0)),
            pl.BlockSpec((CROSS_DIM, d), lambda bi, i: (0, 0)),
        ],
        out_specs=pl.BlockSpec((1, tm, d), lambda bi, i: (bi, i, 0)),
        out_shape=jax.ShapeDtypeStruct((b, s, d), F32),
        compiler_params=_params(("parallel", "parallel")),
        name="out_proj",
    )(x, mix, cross, w1, w2)


def _ffn_kernel(x_ref, g_ref, gf_ref, wg_ref, wu_ref, wd_ref, o_ref, xn_ref, *, final_norm):
    f = pl.program_id(2)

    @pl.when(f == 0)
    def _():
        x = x_ref[0]
        xn_ref[...] = _rmsnorm_f32(x, g_ref[...]).astype(BF16)
        o_ref[0] = x

    xn = xn_ref[...]
    gate = jnp.dot(xn, wg_ref[0], preferred_element_type=F32)
    up = jnp.dot(xn, wu_ref[0], preferred_element_type=F32)
    act = (gate * jax.nn.sigmoid(gate)) * up
    o_ref[0] += jnp.dot(act.astype(BF16), wd_ref[...], preferred_element_type=F32)

    if final_norm:
        @pl.when(f == pl.num_programs(2) - 1)
        def _():
            o_ref[0] = _rmsnorm_f32(o_ref[0], gf_ref[...])


def _ffn(x, g, g_final, w_gate_up, w_down, *, final_norm, tm):
    b, s, d = x.shape
    nf, tf = w_gate_up.shape[0] // 2, w_gate_up.shape[2]
    return pl.pallas_call(
        functools.partial(_ffn_kernel, final_norm=final_norm),
        grid=(b, s // tm, nf),
        in_specs=[
            pl.BlockSpec((1, tm, d), lambda bi, i, f: (bi, i, 0)),
            pl.BlockSpec((1, d), lambda bi, i, f: (0, 0)),
            pl.BlockSpec((1, d), lambda bi, i, f: (0, 0)),
            pl.BlockSpec((1, d, tf), lambda bi, i, f: (f, 0, 0)),
            pl.BlockSpec((1, d, tf), lambda bi, i, f: (nf + f, 0, 0)),
            pl.BlockSpec((tf, d), lambda bi, i, f: (f, 0)),
        ],
        out_specs=pl.BlockSpec((1, tm, d), lambda bi, i, f: (bi, i, 0)),
        out_shape=jax.ShapeDtypeStruct((b, s, d), F32),
        scratch_shapes=[pltpu.VMEM((tm, d), BF16)],
        compiler_params=_params(("parallel", "parallel", "arbitrary")),
        name="ffn",
    )(x, g.reshape(1, d), g_final.reshape(1, d), w_gate_up, w_gate_up, w_down)


TM_PROJ = 1024
TM_OUT = 512
TM_FFN = 1024
TN = 512


def _trunk(x, mem, p):
    for i in range(2):
        kv = _norm_proj(mem, p["norm_mem"][i], p["w_mem_kv"][i], head_w=HEAD_DIM, tm=N_MEM)
        if i == 0:
            projs = _norm_proj(x, p["norm_mix"][i], p["a_w_in"], head_w=HEAD_DIM, tm=TM_PROJ,
                               segments=p["a_segments"])
            mix = _dilated_attention(*projs)
            cross = _cross_attention(projs[0], 3 * A_SLOTS, kv, tm=TM_OUT)
            w1, w2 = p["a_w_out1"], p["a_w_out2"]
        else:
            proj = _norm_proj(x, p["norm_mix"][i], p["b_w_in"], head_w=RET_DK, tm=TM_PROJ)
            mix = _retention(proj, p["b_decay"])
            cross = _cross_attention(proj, 4 * RET_HEADS, kv, tm=TM_OUT)
            w1, w2 = p["b_w_out1"], p["b_w_out2"]
        x = _out_proj(x, mix, cross, w1, w2, tm=TM_OUT)
        x = _ffn(x, p["norm_ffn"][i], p["norm_final"], p["w_gate_up"][i], p["w_down"][i],
                 final_norm=(i == 1), tm=TM_FFN)
    return x


def _column_tiles(w, tn=TN):
    *lead, d, n = w.shape
    w = w.astype(BF16).reshape(*lead, d, n // tn, tn)
    return jnp.swapaxes(w, -3, -2)


def _group_columns(w_in):
    gw = A_SLOTS * HEAD_DIM
    cols, segments = [], []
    for g, (_, dil) in enumerate(DIL_GROUPS):
        group = [w_in[:, (c * N_GROUPS + g) * gw:(c * N_GROUPS + g + 1) * gw] for c in range(3)]
        if g == 0:
            group.append(w_in[:, 3 * N_GROUPS * gw:])
        cols += group
        segments.append((sum(w.shape[1] for w in group) // TN, dil))
    return jnp.concatenate(cols, axis=1), tuple(segments)


def kernel(x_prompt, x_sample, mem_prompt, mem_sample, norm_mix, norm_mem, w_mem_kv, a_w_in, a_w_out,
           b_w_in, b_w_out, b_decay_fwd, b_decay_bwd, norm_ffn, w_gate_up, w_down, norm_final):
    a_mix = A_SLOTS * HEAD_DIM
    b_mix = RET_HEADS * RET_DK
    a_cols, a_segments = _group_columns(a_w_in[0])
    p = {
        "norm_mix": norm_mix, "norm_mem": norm_mem, "norm_ffn": norm_ffn, "norm_final": norm_final,
        "w_mem_kv": _column_tiles(w_mem_kv),
        "a_w_in": _column_tiles(a_cols), "a_segments": a_segments,
        "a_w_out1": a_w_out[0, :a_mix].astype(BF16), "a_w_out2": a_w_out[0, a_mix:].astype(BF16),
        "b_w_in": _column_tiles(b_w_in[0]),
        "b_w_out1": b_w_out[0, :b_mix].astype(BF16), "b_w_out2": b_w_out[0, b_mix:].astype(BF16),
        "b_decay": jnp.stack([b_decay_fwd[0], b_decay_bwd[0]]).astype(F32),
        "w_gate_up": _column_tiles(w_gate_up), "w_down": w_down.astype(BF16),
    }
    return _trunk(x_prompt, mem_prompt, p), _trunk(x_sample, mem_sample, p)
```

```python
import functools

import jax
import jax.numpy as jnp
from jax import lax
from jax.experimental import pallas as pl
from jax.experimental.pallas import tpu as pltpu

F32 = jnp.float32
BF16 = jnp.bfloat16

D_MODEL = 2048
HEAD_DIM = 128
DIL_GROUPS = ((128, 1), (512, 4), (2048, 16))
N_GROUPS = len(DIL_GROUPS)
A_SLOTS = D_MODEL // 256
A_HEADS = N_GROUPS * A_SLOTS
RET_HEADS = D_MODEL // 256
RET_DK = 256
RET_CHUNK = 128
CROSS_HEADS = 4
CROSS_DIM = CROSS_HEADS * HEAD_DIM
N_MEM = 256
D_FF = 5632
EPS = 1e-6
NEG_BIG = -1e30
LOG2_E = 1.4426950408889634
ATT_Q_SCALE = HEAD_DIM ** -0.5 * LOG2_E

ATT_BLOCK = 128
ATT_HALF = 64
ATT_KEYS = 256
ATT_BATCH = 8
DIL_RATIO = 4

V7X_VMEM_LIMIT = 56 * 1024 * 1024
MXU_COLS = 256


def _params(semantics):
    return pltpu.CompilerParams(dimension_semantics=semantics, vmem_limit_bytes=V7X_VMEM_LIMIT)


def _rmsnorm_f32(x, g):
    ms = jnp.mean(x * x, axis=-1, keepdims=True)
    return (x * lax.rsqrt(ms + EPS)) * g


def _norm_proj_kernel(x_ref, g_ref, w_ref, *rest, segments, head_w, half_w, q_steps, q_scale):
    n_seg = len(segments)
    o_refs, xn_ref, scratch = rest[:n_seg], rest[n_seg], rest[n_seg + 1:]
    j = pl.program_id(2)
    tm, tn = xn_ref.shape[0], w_ref.shape[-1]
    per_half = half_w // head_w

    @pl.when(j == 0)
    def _():
        xn_ref[...] = _rmsnorm_f32(x_ref[0], g_ref[...]).astype(BF16)

    def project(half):
        return jnp.dot(xn_ref[...], w_ref[0, :, half * half_w:(half + 1) * half_w], preferred_element_type=F32)

    def deinterleave(o_ref, h, dil):
        stage_ref = scratch[0]
        if dil <= DIL_RATIO:
            for r in range(dil):
                o_ref[0, h, r] = stage_ref[h, pl.ds(r, tm // dil, stride=dil), :].astype(o_ref.dtype)
            return
        stage2_ref = scratch[1]
        outer = dil // DIL_RATIO
        for r1 in range(DIL_RATIO):
            stage2_ref[h, r1] = stage_ref[h, pl.ds(r1, tm // DIL_RATIO, stride=DIL_RATIO), :]
        for r1 in range(DIL_RATIO):
            for k in range(outer):
                o_ref[0, h, k * DIL_RATIO + r1] = (
                    stage2_ref[h, r1, pl.ds(k, tm // dil, stride=outer), :].astype(o_ref.dtype))

    start = 0
    for o_ref, (steps, dil) in zip(o_refs, segments):
        def segment(o_ref=o_ref, dil=dil, start=start):
            for half in range(tn // half_w):
                res = project(half)
                if q_steps:
                    res = res * jnp.where(j - start < q_steps, q_scale, 1.0)
                for hh in range(per_half):
                    h = half * per_half + hh
                    cols = res[:, hh * head_w:(hh + 1) * head_w]
                    if dil == 1:
                        o_ref[0, h] = cols.astype(o_ref.dtype)
                    else:
                        scratch[0][h] = cols
                if dil > 1:
                    for hh in range(per_half):
                        deinterleave(o_ref, half * per_half + hh, dil)

        if n_seg == 1:
            segment()
        else:
            pl.when((j >= start) & (j < start + steps))(segment)
        start += steps


def _norm_proj(x, g, w_tiles, *, head_w, tm, segments=None, q_steps=0, q_scale=1.0):
    b, s, d = x.shape
    nt, _, tn = w_tiles.shape
    hps = tn // head_w
    segments = segments or ((nt, 1),)
    max_dil = max(dil for _, dil in segments)
    scratch = [pltpu.VMEM((tm, d), BF16)]
    if max_dil > 1:
        scratch.append(pltpu.VMEM((hps, tm, head_w), F32))
    if max_dil > DIL_RATIO:
        scratch.append(pltpu.VMEM((hps, DIL_RATIO, tm // DIL_RATIO, head_w), F32))
    out_specs, out_shapes, start = [], [], 0
    for steps, dil in segments:
        def held(j, start=start, steps=steps):
            return jnp.clip(j - start, 0, steps - 1)
        if dil == 1:
            out_specs.append(pl.BlockSpec((1, hps, tm, head_w), lambda bi, i, j, held=held: (bi, held(j), i, 0)))
            out_shapes.append(jax.ShapeDtypeStruct((b, steps * hps, s, head_w), BF16))
        else:
            out_specs.append(pl.BlockSpec((1, hps, dil, tm // dil, head_w),
                                          lambda bi, i, j, held=held: (bi, held(j), 0, i, 0)))
            out_shapes.append(jax.ShapeDtypeStruct((b, steps * hps, dil, s // dil, head_w), BF16))
        start += steps
    kern = functools.partial(_norm_proj_kernel, segments=tuple(segments), head_w=head_w,
                             half_w=max(head_w, MXU_COLS), q_steps=q_steps, q_scale=q_scale)
    outs = pl.pallas_call(
        kern,
        grid=(b, s // tm, nt),
        in_specs=[
            pl.BlockSpec((1, tm, d), lambda bi, i, j: (bi, i, 0)),
            pl.BlockSpec((1, d), lambda bi, i, j: (0, 0)),
            pl.BlockSpec((1, d, tn), lambda bi, i, j: (j, 0, 0)),
        ],
        out_specs=out_specs,
        out_shape=out_shapes,
        scratch_shapes=scratch,
        compiler_params=_params(("parallel", "parallel", "arbitrary")),
        name="norm_proj",
    )(x, g.reshape(1, d), w_tiles)
    return outs if len(outs) > 1 else outs[0]


def _dil_attn_kernel(q0, k0, v0, q1, k1, v1, q2, k2, v2, o_ref,
                     m1_ref, l1_ref, a1_ref, m0_ref, l0_ref, a0_ref, bias0_ref, bias1_ref, bias2_ref,
                     *, seq, unroll):
    h = pl.program_id(1)
    groups = ((q0, k0, v0, bias0_ref), (q1, k1, v1, bias1_ref), (q2, k2, v2, bias2_ref))
    stat = (ATT_BLOCK, HEAD_DIM)

    def build_bias(g):
        bias_ref = groups[g][3]
        dil = DIL_GROUPS[g][1]
        kw = bias_ref.shape[-1]
        jf = (h + (g * A_SLOTS + 1)).astype(F32)
        slope = jnp.exp2(jnp.full((ATT_BLOCK, kw), -8.0 / A_HEADS, F32) * jf)
        row = lax.broadcasted_iota(jnp.int32, (ATT_BLOCK, kw), 0)
        col = lax.broadcasted_iota(jnp.int32, (ATT_BLOCK, kw), 1)
        for t, delta in enumerate((0, -ATT_HALF, ATT_BLOCK - kw)):
            arel = jnp.abs(col - row + delta)
            bias_ref[t] = jnp.where(arel <= ATT_HALF, (-slope * (dil * arel).astype(F32)) * LOG2_E, NEG_BIG)

    def scores(g, q, k, blk, nblk):
        bias_ref = groups[g][3]
        table = jnp.where(blk == 0, 0, jnp.where(blk == nblk - 1, 2, 1))
        return lax.dot_general(q, k, (((1,), (1,)), ((), ())), preferred_element_type=F32) + bias_ref[table]

    def accumulate(s, v, state):
        m_new = jnp.broadcast_to(jnp.max(s, axis=-1, keepdims=True), stat)
        if state is not None:
            m_old, l_old, acc_old = state
            m_new = jnp.maximum(m_old, m_new)
            alpha = jnp.exp2(m_old - m_new)
        slabs = [s[:, c:c + HEAD_DIM] - m_new for c in range(0, s.shape[1], HEAD_DIM)]
        p = jnp.exp2(slabs[0] if len(slabs) == 1 else jnp.concatenate(slabs, axis=1))
        l_new = jnp.broadcast_to(jnp.sum(p, axis=-1, keepdims=True), stat)
        acc_new = jnp.dot(p.astype(BF16), v, preferred_element_type=F32)
        if state is not None:
            l_new = alpha * l_old + l_new
            acc_new = alpha * acc_old + acc_new
        return m_new, l_new, acc_new

    def run_group(g):
        q_ref, k_ref, v_ref, bias_ref = groups[g]
        dil = DIL_GROUPS[g][1]
        length = seq // dil
        kw = bias_ref.shape[-1]
        nblk = length // ATT_BLOCK
        shift = nblk.bit_length() - 1

        def score_block(it):
            r = it >> shift
            blk = it & (nblk - 1)
            i0 = pl.multiple_of(blk * ATT_BLOCK, ATT_BLOCK)
            ws = pl.multiple_of(jnp.clip(i0 - ATT_HALF, 0, length - kw), ATT_HALF)
            if g == 0:
                q = q_ref[0, 0, pl.ds(i0, ATT_BLOCK), :]
                k = k_ref[0, 0, pl.ds(ws, kw), :]
                v = v_ref[0, 0, pl.ds(ws, kw), :]
            else:
                q = q_ref[0, 0, r, pl.ds(i0, ATT_BLOCK), :]
                k = k_ref[0, 0, r, pl.ds(ws, kw), :]
                v = v_ref[0, 0, r, pl.ds(ws, kw), :]
            return r, i0, scores(g, q, k, blk, nblk), v

        def finish_block(r, i0, s, v):
            if g == 2:
                m, l, acc = accumulate(s, v, None)
                rows = pl.ds(i0 * DIL_RATIO + (r >> 2), ATT_BLOCK, stride=DIL_RATIO)
                m1_ref[r & 3, rows, :] = m
                l1_ref[r & 3, rows, :] = l
                a1_ref[r & 3, rows, :] = acc
            elif g == 1:
                rows = pl.ds(i0, ATT_BLOCK)
                m, l, acc = accumulate(s, v, (m1_ref[r, rows, :], l1_ref[r, rows, :], a1_ref[r, rows, :]))
                rows = pl.ds(i0 * DIL_RATIO + r, ATT_BLOCK, stride=DIL_RATIO)
                m0_ref[rows, :] = m
                l0_ref[rows, :] = l
                a0_ref[rows, :] = acc
            else:
                rows = pl.ds(i0, ATT_BLOCK)
                _, l, acc = accumulate(s, v, (m0_ref[rows, :], l0_ref[rows, :], a0_ref[rows, :]))
                o_ref[0, rows, :] = (acc / l).astype(o_ref.dtype)

        total = dil * nblk
        batch = min(ATT_BATCH, total)

        def body(bi, carry):
            blocks = [score_block(bi * batch + i) for i in range(batch)]
            for blk_args in blocks:
                finish_block(*blk_args)
            return carry

        lax.fori_loop(0, total // batch, body, 0, unroll=max(1, min(unroll, total) // batch))

    for g in range(N_GROUPS):
        build_bias(g)
    run_group(2)
    run_group(1)
    run_group(0)


def _dilated_attention(p0, p1, p2, *, unroll=32):
    b, _, s, hd = p0.shape
    args, specs = [], []
    for c in range(3):
        args.append(p0)
        specs.append(pl.BlockSpec((1, 1, s, hd), lambda bi, hi, c=c: (bi, c * A_SLOTS + hi, 0, 0)))
    for pg in (p1, p2):
        dil, length = pg.shape[2], pg.shape[3]
        for c in range(3):
            args.append(pg)
            specs.append(pl.BlockSpec((1, 1, dil, length, hd), lambda bi, hi, c=c: (bi, c * A_SLOTS + hi, 0, 0, 0)))
    l1 = s // DIL_GROUPS[1][1]
    bias = [pltpu.VMEM((3, ATT_BLOCK, min(ATT_KEYS, s // dil)), F32) for _, dil in DIL_GROUPS]
    return pl.pallas_call(
        functools.partial(_dil_attn_kernel, seq=s, unroll=unroll),
        grid=(b, A_SLOTS),
        in_specs=specs,
        out_specs=pl.BlockSpec((1, s, hd), lambda bi, hi: (bi, 0, hi)),
        out_shape=jax.ShapeDtypeStruct((b, s, A_SLOTS * hd), BF16),
        scratch_shapes=[pltpu.VMEM((DIL_RATIO, l1, hd), F32)] * 3 + [pltpu.VMEM((s, hd), F32)] * 3 + bias,
        compiler_params=_params(("parallel", "parallel")),
        name="dilated_attention",
    )(*args)


def _retention_kernel(dec_ref, q_ref, k_ref, v_ref, gate_ref, o_ref,
                      yf_ref, yb_ref, sf_ref, sb_ref, qdf_ref, kdf_ref, qdb_ref, kdb_ref, intra_ref,
                      *, seq, unroll):
    h = pl.program_id(1)
    c = RET_CHUNK
    n = seq // c
    dk = RET_DK

    def log_gamma(e, shape):
        return jnp.log1p(-jnp.exp2(-jnp.full(shape, e, F32)))

    e_f = dec_ref[0, h]
    e_b = dec_ref[1, h]
    lg_f = log_gamma(e_f, (c, dk))
    lg_b = log_gamma(e_b, (c, dk))
    i_row = lax.broadcasted_iota(jnp.int32, (c, dk), 0).astype(F32)
    qdf_ref[...] = jnp.exp((i_row + 1.0) * lg_f)
    kdf_ref[...] = jnp.exp((c - 1.0 - i_row) * lg_f)
    qdb_ref[...] = jnp.exp((c - i_row) * lg_b)
    kdb_ref[...] = jnp.exp(i_row * lg_b)
    cd_f = jnp.exp(c * log_gamma(e_f, (1, dk)))
    cd_b = jnp.exp(c * log_gamma(e_b, (1, dk)))
    t = lax.broadcasted_iota(jnp.int32, (c, c), 0)
    s = lax.broadcasted_iota(jnp.int32, (c, c), 1)
    diff = (t - s).astype(F32)
    intra_ref[...] = jnp.where(t >= s, jnp.exp(jnp.where(t >= s, diff, 0.0) * log_gamma(e_f, (c, c))),
                               jnp.exp(jnp.where(t < s, -diff, 0.0) * log_gamma(e_b, (c, c))))

    def load(ci):
        rows = pl.ds(pl.multiple_of(ci * c, c), c)
        q = q_ref[0, 0, rows, :].astype(F32) * (dk ** -0.5)
        k = k_ref[0, 0, rows, :].astype(F32)
        v = v_ref[0, 0, rows, :]
        return rows, q, k, v

    def inter(q_dec, k_dec, v, st_ref, cd):
        y = jnp.dot(q_dec.astype(BF16), st_ref[...].astype(BF16), preferred_element_type=F32)
        kv = lax.dot_general(k_dec.astype(BF16), v, (((0,), (0,)), ((), ())), preferred_element_type=F32)
        st_ref[...] = cd * st_ref[...] + kv
        return y

    sf_ref[...] = jnp.zeros_like(sf_ref)
    sb_ref[...] = jnp.zeros_like(sb_ref)

    def step(ci, carry):
        rows, q, k, v = load(ci)
        sc = lax.dot_general(q.astype(BF16), k.astype(BF16), (((1,), (1,)), ((), ())),
                             preferred_element_type=F32) * intra_ref[...]
        y = jnp.dot(sc.astype(BF16), v, preferred_element_type=F32)
        yf_ref[rows, :] = y + inter(q * qdf_ref[...], k * kdf_ref[...], v, sf_ref, cd_f)
        rows, q, k, v = load(n - 1 - ci)
        yb_ref[rows, :] = inter(q * qdb_ref[...], k * kdb_ref[...], v, sb_ref, cd_b)
        return carry

    lax.fori_loop(0, n, step, 0, unroll=unroll)

    def finish(ci, carry):
        rows = pl.ds(pl.multiple_of(ci * c, c), c)
        y = yf_ref[rows, :] + yb_ref[rows, :]
        y = y * lax.rsqrt(jnp.mean(y * y, axis=-1, keepdims=True) + EPS)
        gate = gate_ref[0, 0, rows, :].astype(F32)
        o_ref[0, rows, :] = (y * (gate * jax.nn.sigmoid(gate))).astype(o_ref.dtype)
        return carry

    lax.fori_loop(0, n, finish, 0, unroll=unroll)


def _retention(proj, decays, *, unroll=8):
    b, _, s, dk = proj.shape
    c = RET_CHUNK

    def head_spec(offset):
        return pl.BlockSpec((1, 1, s, dk), lambda bi, hi: (bi, offset + hi, 0, 0))

    return pl.pallas_call(
        functools.partial(_retention_kernel, seq=s, unroll=unroll),
        grid=(b, RET_HEADS),
        in_specs=[pl.BlockSpec(memory_space=pltpu.SMEM)] + [head_spec(RET_HEADS * i) for i in range(4)],
        out_specs=pl.BlockSpec((1, s, dk), lambda bi, hi: (bi, 0, hi)),
        out_shape=jax.ShapeDtypeStruct((b, s, RET_HEADS * dk), BF16),
        scratch_shapes=[pltpu.VMEM((s, dk), F32)] * 2 + [pltpu.VMEM((dk, dk), F32)] * 2
        + [pltpu.VMEM((c, dk), F32)] * 4 + [pltpu.VMEM((c, c), F32)],
        compiler_params=_params(("parallel", "parallel")),
        name="retention",
    )(decays, proj, proj, proj, proj)


def _cross_kernel(q_ref, kv_ref, o_ref, *, q_w):
    scale = HEAD_DIM ** -0.5
    per = q_w // HEAD_DIM
    for hh in range(CROSS_HEADS):
        q = q_ref[0, hh // per][:, (hh % per) * HEAD_DIM:(hh % per + 1) * HEAD_DIM]
        k = kv_ref[0, hh]
        v = kv_ref[0, CROSS_HEADS + hh]
        s = lax.dot_general(q, k, (((1,), (1,)), ((), ())), preferred_element_type=F32) * scale
        m = jnp.max(s, axis=-1, keepdims=True)
        p = jnp.exp(s - m)
        l = jnp.sum(p, axis=-1, keepdims=True)
        o = jnp.dot(p.astype(BF16), v, preferred_element_type=F32) / l
        o_ref[0, :, hh * HEAD_DIM:(hh + 1) * HEAD_DIM] = o.astype(o_ref.dtype)


def _cross_attention(proj, q_head0, kv, *, tm):
    b, _, s, q_w = proj.shape
    nq = CROSS_DIM // q_w
    return pl.pallas_call(
        functools.partial(_cross_kernel, q_w=q_w),
        grid=(b, s // tm),
        in_specs=[
            pl.BlockSpec((1, nq, tm, q_w), lambda bi, i: (bi, q_head0 // nq, i, 0)),
            pl.BlockSpec((1, 2 * CROSS_HEADS, N_MEM, HEAD_DIM), lambda bi, i: (bi, 0, 0, 0)),
        ],
        out_specs=pl.BlockSpec((1, tm, CROSS_DIM), lambda bi, i: (bi, i, 0)),
        out_shape=jax.ShapeDtypeStruct((b, s, CROSS_DIM), BF16),
        compiler_params=_params(("parallel", "parallel")),
        name="cross_attention",
    )(proj, kv)


def _out_proj_kernel(x_ref, mix_ref, cross_ref, w1_ref, w2_ref, o_ref):
    acc = jnp.dot(mix_ref[0], w1_ref[...], preferred_element_type=F32)
    acc = acc + jnp.dot(cross_ref[0], w2_ref[...], preferred_element_type=F32)
    o_ref[0] = x_ref[0] + acc


def _out_proj(x, mix, cross, w1, w2, *, tm):
    b, s, d = x.shape
    km = mix.shape[-1]
    return pl.pallas_call(
        _out_proj_kernel,
        grid=(b, s // tm),
        in_specs=[
            pl.BlockSpec((1, tm, d), lambda bi, i: (bi, i, 0)),
            pl.BlockSpec((1, tm, km), lambda bi, i: (bi, i, 0)),
            pl.BlockSpec((1, tm, CROSS_DIM), lambda bi, i: (bi, i, 0)),
            pl.BlockSpec((km, d), lambda bi, i: (0, 0)),
            pl.BlockSpec((CROSS_DIM, d), lambda bi, i: (0, 0)),
        ],
        out_specs=pl.BlockSpec((1, tm, d), lambda bi, i: (bi, i, 0)),
        out_shape=jax.ShapeDtypeStruct((b, s, d), F32),
        compiler_params=_params(("parallel", "parallel")),
        name="out_proj",
    )(x, mix, cross, w1, w2)


def _ffn_kernel(x_ref, g_ref, gf_ref, wg_ref, wu_ref, wd_ref, o_ref, xn_ref, *, final_norm):
    f = pl.program_id(2)

    @pl.when(f == 0)
    def _():
        x = x_ref[0]
        xn_ref[...] = _rmsnorm_f32(x, g_ref[...]).astype(BF16)
        o_ref[0] = x

    xn = xn_ref[...]
    gate = jnp.dot(xn, wg_ref[0], preferred_element_type=F32)
    up = jnp.dot(xn, wu_ref[0], preferred_element_type=F32)
    act = (gate * jax.nn.sigmoid(gate)) * up
    o_ref[0] += jnp.dot(act.astype(BF16), wd_ref[...], preferred_element_type=F32)

    if final_norm:
        @pl.when(f == pl.num_programs(2) - 1)
        def _():
            o_ref[0] = _rmsnorm_f32(o_ref[0], gf_ref[...])


def _ffn(x, g, g_final, w_gate_up, w_down, *, final_norm, tm):
    b, s, d = x.shape
    nf, tf = w_gate_up.shape[0] // 2, w_gate_up.shape[2]
    return pl.pallas_call(
        functools.partial(_ffn_kernel, final_norm=final_norm),
        grid=(b, s // tm, nf),
        in_specs=[
            pl.BlockSpec((1, tm, d), lambda bi, i, f: (bi, i, 0)),
            pl.BlockSpec((1, d), lambda bi, i, f: (0, 0)),
            pl.BlockSpec((1, d), lambda bi, i, f: (0, 0)),
            pl.BlockSpec((1, d, tf), lambda bi, i, f: (f, 0, 0)),
            pl.BlockSpec((1, d, tf), lambda bi, i, f: (nf + f, 0, 0)),
            pl.BlockSpec((tf, d), lambda bi, i, f: (f, 0)),
        ],
        out_specs=pl.BlockSpec((1, tm, d), lambda bi, i, f: (bi, i, 0)),
        out_shape=jax.ShapeDtypeStruct((b, s, d), F32),
        scratch_shapes=[pltpu.VMEM((tm, d), BF16)],
        compiler_params=_params(("parallel", "parallel", "arbitrary")),
        name="ffn",
    )(x, g.reshape(1, d), g_final.reshape(1, d), w_gate_up, w_gate_up, w_down)


TM_PROJ = 1024
TM_PROJ_B = 2048
TM_OUT = 512
TM_FFN = 1024
TN = 512


def _trunk(x, mem, p):
    for i in range(2):
        kv = _norm_proj(mem, p["norm_mem"][i], p["w_mem_kv"][i], head_w=HEAD_DIM, tm=N_MEM)
        if i == 0:
            projs = _norm_proj(x, p["norm_mix"][i], p["a_w_in"], head_w=HEAD_DIM, tm=TM_PROJ,
                               segments=p["a_segments"], q_steps=A_SLOTS * HEAD_DIM // TN, q_scale=ATT_Q_SCALE)
            mix = _dilated_attention(*projs)
            cross = _cross_attention(projs[0], 3 * A_SLOTS, kv, tm=TM_OUT)
            w1, w2 = p["a_w_out1"], p["a_w_out2"]
        else:
            proj = _norm_proj(x, p["norm_mix"][i], p["b_w_in"], head_w=RET_DK, tm=TM_PROJ_B)
            mix = _retention(proj, p["b_decay"])
            cross = _cross_attention(proj, 4 * RET_HEADS, kv, tm=TM_OUT)
            w1, w2 = p["b_w_out1"], p["b_w_out2"]
        x = _out_proj(x, mix, cross, w1, w2, tm=TM_OUT)
        x = _ffn(x, p["norm_ffn"][i], p["norm_final"], p["w_gate_up"][i], p["w_down"][i],
                 final_norm=(i == 1), tm=TM_FFN)
    return x


def _column_tiles(w, tn=TN):
    *lead, d, n = w.shape
    w = w.astype(BF16).reshape(*lead, d, n // tn, tn)
    return jnp.swapaxes(w, -3, -2)


def _group_columns(w_in):
    gw = A_SLOTS * HEAD_DIM
    cols, segments = [], []
    for g, (_, dil) in enumerate(DIL_GROUPS):
        group = [w_in[:, (c * N_GROUPS + g) * gw:(c * N_GROUPS + g + 1) * gw] for c in range(3)]
        if g == 0:
            group.append(w_in[:, 3 * N_GROUPS * gw:])
        cols += group
        segments.append((sum(w.shape[1] for w in group) // TN, dil))
    return jnp.concatenate(cols, axis=1), tuple(segments)


def kernel(x_prompt, x_sample, mem_prompt, mem_sample, norm_mix, norm_mem, w_mem_kv, a_w_in, a_w_out,
           b_w_in, b_w_out, b_decay_fwd, b_decay_bwd, norm_ffn, w_gate_up, w_down, norm_final):
    a_mix = A_SLOTS * HEAD_DIM
    b_mix = RET_HEADS * RET_DK
    a_cols, a_segments = _group_columns(a_w_in[0])
    p = {
        "norm_mix": norm_mix, "norm_mem": norm_mem, "norm_ffn": norm_ffn, "norm_final": norm_final,
        "w_mem_kv": _column_tiles(w_mem_kv),
        "a_w_in": _column_tiles(a_cols), "a_segments": a_segments,
        "a_w_out1": a_w_out[0, :a_mix].astype(BF16), "a_w_out2": a_w_out[0, a_mix:].astype(BF16),
        "b_w_in": _column_tiles(b_w_in[0]),
        "b_w_out1": b_w_out[0, :b_mix].astype(BF16), "b_w_out2": b_w_out[0, b_mix:].astype(BF16),
        "b_decay": jnp.stack([b_decay_fwd[0], b_decay_bwd[0]]).astype(F32),
        "w_gate_up": _column_tiles(w_gate_up), "w_down": w_down.astype(BF16),
    }
    return _trunk(x_prompt, mem_prompt, p), _trunk(x_sample, mem_sample, p)
```

```python
import functools

import jax
import jax.numpy as jnp
from jax import lax
from jax.experimental import pallas as pl
from jax.experimental.pallas import tpu as pltpu

F32 = jnp.float32
BF16 = jnp.bfloat16

D_MODEL = 2048
HEAD_DIM = 128
DIL_GROUPS = ((128, 1), (512, 4), (2048, 16))
N_GROUPS = len(DIL_GROUPS)
A_SLOTS = D_MODEL // 256
A_HEADS = N_GROUPS * A_SLOTS
RET_HEADS = D_MODEL // 256
RET_DK = 256
RET_CHUNK = 128
CROSS_HEADS = 4
CROSS_DIM = CROSS_HEADS * HEAD_DIM
N_MEM = 256
D_FF = 5632
EPS = 1e-6
NEG_BIG = -1e30
LOG2_E = 1.4426950408889634
ATT_Q_SCALE = HEAD_DIM ** -0.5 * LOG2_E

ATT_BLOCK = 128
ATT_HALF = 64
ATT_KEYS = 256
ATT_BATCH = 8
DIL_RATIO = 4

TM_PROJ = 1024
TM_PROJ_B = 2048
TM_OUT = 512
TM_FFN = 1024
TN = 512

V7X_VMEM_LIMIT = 56 * 1024 * 1024
MXU_COLS = 256


def _params(semantics):
    return pltpu.CompilerParams(dimension_semantics=semantics, vmem_limit_bytes=V7X_VMEM_LIMIT)


def _rmsnorm_f32(x, g):
    ms = jnp.mean(x * x, axis=-1, keepdims=True)
    return (x * lax.rsqrt(ms + EPS)) * g


def _norm_proj_kernel(x_ref, g_ref, w_ref, *rest, segments, head_w, half_w, q_steps, q_scale):
    n_seg = len(segments)
    o_refs, xn_ref, scratch = rest[:n_seg], rest[n_seg], rest[n_seg + 1:]
    j = pl.program_id(2)
    tm, tn = xn_ref.shape[0], w_ref.shape[-1]
    per_half = half_w // head_w

    @pl.when(j == 0)
    def _():
        xn_ref[...] = _rmsnorm_f32(x_ref[0], g_ref[...]).astype(BF16)

    def project(half):
        return jnp.dot(xn_ref[...], w_ref[:, half * half_w:(half + 1) * half_w], preferred_element_type=F32)

    def deinterleave(o_ref, h, dil):
        stage_ref = scratch[0]
        if dil <= DIL_RATIO:
            for r in range(dil):
                o_ref[0, h, r] = stage_ref[h, pl.ds(r, tm // dil, stride=dil), :].astype(o_ref.dtype)
            return
        stage2_ref = scratch[1]
        outer = dil // DIL_RATIO
        for r1 in range(DIL_RATIO):
            stage2_ref[h, r1] = stage_ref[h, pl.ds(r1, tm // DIL_RATIO, stride=DIL_RATIO), :]
        for r1 in range(DIL_RATIO):
            for k in range(outer):
                o_ref[0, h, k * DIL_RATIO + r1] = (
                    stage2_ref[h, r1, pl.ds(k, tm // dil, stride=outer), :].astype(o_ref.dtype))

    start = 0
    for o_ref, (steps, dil) in zip(o_refs, segments):
        def segment(o_ref=o_ref, dil=dil, start=start):
            for half in range(tn // half_w):
                res = project(half)
                if q_steps:
                    res = res * jnp.where(j - start < q_steps, q_scale, 1.0)
                for hh in range(per_half):
                    h = half * per_half + hh
                    cols = res[:, hh * head_w:(hh + 1) * head_w]
                    if dil == 1:
                        o_ref[0, h] = cols.astype(o_ref.dtype)
                    else:
                        scratch[0][h] = cols
                if dil > 1:
                    for hh in range(per_half):
                        deinterleave(o_ref, half * per_half + hh, dil)

        if n_seg == 1:
            segment()
        else:
            pl.when((j >= start) & (j < start + steps))(segment)
        start += steps


def _norm_proj(x, g, w, *, head_w, tm, tn=TN, segments=None, q_steps=0, q_scale=1.0):
    b, s, d = x.shape
    nt = w.shape[1] // tn
    hps = tn // head_w
    segments = segments or ((nt, 1),)
    max_dil = max(dil for _, dil in segments)
    scratch = [pltpu.VMEM((tm, d), BF16)]
    if max_dil > 1:
        scratch.append(pltpu.VMEM((hps, tm, head_w), F32))
    if max_dil > DIL_RATIO:
        scratch.append(pltpu.VMEM((hps, DIL_RATIO, tm // DIL_RATIO, head_w), F32))
    out_specs, out_shapes, start = [], [], 0
    for steps, dil in segments:
        def held(j, start=start, steps=steps):
            return jnp.clip(j - start, 0, steps - 1)
        if dil == 1:
            out_specs.append(pl.BlockSpec((1, hps, tm, head_w), lambda bi, i, j, held=held: (bi, held(j), i, 0)))
            out_shapes.append(jax.ShapeDtypeStruct((b, steps * hps, s, head_w), BF16))
        else:
            out_specs.append(pl.BlockSpec((1, hps, dil, tm // dil, head_w),
                                          lambda bi, i, j, held=held: (bi, held(j), 0, i, 0)))
            out_shapes.append(jax.ShapeDtypeStruct((b, steps * hps, dil, s // dil, head_w), BF16))
        start += steps
    kern = functools.partial(_norm_proj_kernel, segments=tuple(segments), head_w=head_w,
                             half_w=max(head_w, MXU_COLS), q_steps=q_steps, q_scale=q_scale)
    outs = pl.pallas_call(
        kern,
        grid=(b, s // tm, nt),
        in_specs=[
            pl.BlockSpec((1, tm, d), lambda bi, i, j: (bi, i, 0)),
            pl.BlockSpec((1, d), lambda bi, i, j: (0, 0)),
            pl.BlockSpec((d, tn), lambda bi, i, j: (0, j)),
        ],
        out_specs=out_specs,
        out_shape=out_shapes,
        scratch_shapes=scratch,
        compiler_params=_params(("parallel", "parallel", "arbitrary")),
        name="norm_proj",
    )(x, g.reshape(1, d), w)
    return outs if len(outs) > 1 else outs[0]


def _dil_attn_kernel(q0, k0, v0, q1, k1, v1, q2, k2, v2, o_ref,
                     m1_ref, l1_ref, a1_ref, m0_ref, l0_ref, a0_ref, bias0_ref, bias1_ref, bias2_ref,
                     *, seq, unroll):
    h = pl.program_id(1)
    groups = ((q0, k0, v0, bias0_ref), (q1, k1, v1, bias1_ref), (q2, k2, v2, bias2_ref))
    stat = (ATT_BLOCK, HEAD_DIM)

    def build_bias(g):
        bias_ref = groups[g][3]
        dil = DIL_GROUPS[g][1]
        kw = bias_ref.shape[-1]
        jf = (h + (g * A_SLOTS + 1)).astype(F32)
        slope = jnp.exp2(jnp.full((ATT_BLOCK, kw), -8.0 / A_HEADS, F32) * jf)
        row = lax.broadcasted_iota(jnp.int32, (ATT_BLOCK, kw), 0)
        col = lax.broadcasted_iota(jnp.int32, (ATT_BLOCK, kw), 1)
        for t, delta in enumerate((0, -ATT_HALF, ATT_BLOCK - kw)):
            arel = jnp.abs(col - row + delta)
            bias_ref[t] = jnp.where(arel <= ATT_HALF, (-slope * (dil * arel).astype(F32)) * LOG2_E, NEG_BIG)

    def scores(g, q, k, blk, nblk):
        bias_ref = groups[g][3]
        table = jnp.where(blk == 0, 0, jnp.where(blk == nblk - 1, 2, 1))
        return lax.dot_general(q, k, (((1,), (1,)), ((), ())), preferred_element_type=F32) + bias_ref[table]

    def accumulate(s, v, state):
        m_new = jnp.broadcast_to(jnp.max(s, axis=-1, keepdims=True), stat)
        if state is not None:
            m_old, l_old, acc_old = state
            m_new = jnp.maximum(m_old, m_new)
            alpha = jnp.exp2(m_old - m_new)
        slabs = [s[:, c:c + HEAD_DIM] - m_new for c in range(0, s.shape[1], HEAD_DIM)]
        p = jnp.exp2(slabs[0] if len(slabs) == 1 else jnp.concatenate(slabs, axis=1))
        l_new = jnp.broadcast_to(jnp.sum(p, axis=-1, keepdims=True), stat)
        acc_new = jnp.dot(p.astype(BF16), v, preferred_element_type=F32)
        if state is not None:
            l_new = alpha * l_old + l_new
            acc_new = alpha * acc_old + acc_new
        return m_new, l_new, acc_new

    def run_group(g):
        q_ref, k_ref, v_ref, bias_ref = groups[g]
        dil = DIL_GROUPS[g][1]
        length = seq // dil
        kw = bias_ref.shape[-1]
        nblk = length // ATT_BLOCK
        shift = nblk.bit_length() - 1

        def score_block(it):
            r = it >> shift
            blk = it & (nblk - 1)
            i0 = pl.multiple_of(blk * ATT_BLOCK, ATT_BLOCK)
            ws = pl.multiple_of(jnp.clip(i0 - ATT_HALF, 0, length - kw), ATT_HALF)
            if g == 0:
                q = q_ref[0, 0, pl.ds(i0, ATT_BLOCK), :]
                k = k_ref[0, 0, pl.ds(ws, kw), :]
                v = v_ref[0, 0, pl.ds(ws, kw), :]
            else:
                q = q_ref[0, 0, r, pl.ds(i0, ATT_BLOCK), :]
                k = k_ref[0, 0, r, pl.ds(ws, kw), :]
                v = v_ref[0, 0, r, pl.ds(ws, kw), :]
            return r, i0, scores(g, q, k, blk, nblk), v

        def finish_block(r, i0, s, v):
            if g == 2:
                m, l, acc = accumulate(s, v, None)
                rows = pl.ds(i0 * DIL_RATIO + (r >> 2), ATT_BLOCK, stride=DIL_RATIO)
                m1_ref[r & 3, rows, :] = m
                l1_ref[r & 3, rows, :] = l
                a1_ref[r & 3, rows, :] = acc
            elif g == 1:
                rows = pl.ds(i0, ATT_BLOCK)
                m, l, acc = accumulate(s, v, (m1_ref[r, rows, :], l1_ref[r, rows, :], a1_ref[r, rows, :]))
                rows = pl.ds(i0 * DIL_RATIO + r, ATT_BLOCK, stride=DIL_RATIO)
                m0_ref[rows, :] = m
                l0_ref[rows, :] = l
                a0_ref[rows, :] = acc
            else:
                rows = pl.ds(i0, ATT_BLOCK)
                _, l, acc = accumulate(s, v, (m0_ref[rows, :], l0_ref[rows, :], a0_ref[rows, :]))
                o_ref[0, rows, :] = (acc / l).astype(o_ref.dtype)

        total = dil * nblk
        batch = min(ATT_BATCH, total)

        def body(bi, carry):
            blocks = [score_block(bi * batch + i) for i in range(batch)]
            for blk_args in blocks:
                finish_block(*blk_args)
            return carry

        lax.fori_loop(0, total // batch, body, 0, unroll=max(1, min(unroll, total) // batch))

    for g in range(N_GROUPS):
        build_bias(g)
    run_group(2)
    run_group(1)
    run_group(0)


def _dilated_attention(p0, p1, p2, *, unroll=32):
    b, _, s, hd = p0.shape
    args, specs = [], []
    for c in range(3):
        args.append(p0)
        specs.append(pl.BlockSpec((1, 1, s, hd), lambda bi, hi, c=c: (bi, c * A_SLOTS + hi, 0, 0)))
    for pg in (p1, p2):
        dil, length = pg.shape[2], pg.shape[3]
        for c in range(3):
            args.append(pg)
            specs.append(pl.BlockSpec((1, 1, dil, length, hd), lambda bi, hi, c=c: (bi, c * A_SLOTS + hi, 0, 0, 0)))
    l1 = s // DIL_GROUPS[1][1]
    bias = [pltpu.VMEM((3, ATT_BLOCK, min(ATT_KEYS, s // dil)), F32) for _, dil in DIL_GROUPS]
    return pl.pallas_call(
        functools.partial(_dil_attn_kernel, seq=s, unroll=unroll),
        grid=(b, A_SLOTS),
        in_specs=specs,
        out_specs=pl.BlockSpec((1, s, hd), lambda bi, hi: (bi, 0, hi)),
        out_shape=jax.ShapeDtypeStruct((b, s, A_SLOTS * hd), BF16),
        scratch_shapes=[pltpu.VMEM((DIL_RATIO, l1, hd), F32)] * 3 + [pltpu.VMEM((s, hd), F32)] * 3 + bias,
        compiler_params=_params(("parallel", "parallel")),
        name="dilated_attention",
    )(*args)


def _retention_kernel(dec_ref, q_ref, k_ref, v_ref, gate_ref, o_ref,
                      yf_ref, yb_ref, sf_ref, sb_ref, qdf_ref, kdf_ref, qdb_ref, kdb_ref, intra_ref,
                      *, seq, unroll):
    h = pl.program_id(1)
    c = RET_CHUNK
    n = seq // c
    dk = RET_DK

    def log_gamma(e, shape):
        return jnp.log1p(-jnp.exp2(-jnp.full(shape, e, F32)))

    e_f = dec_ref[0, h]
    e_b = dec_ref[1, h]
    lg_f = log_gamma(e_f, (c, dk))
    lg_b = log_gamma(e_b, (c, dk))
    i_row = lax.broadcasted_iota(jnp.int32, (c, dk), 0).astype(F32)
    qdf_ref[...] = jnp.exp((i_row + 1.0) * lg_f)
    kdf_ref[...] = jnp.exp((c - 1.0 - i_row) * lg_f)
    qdb_ref[...] = jnp.exp((c - i_row) * lg_b)
    kdb_ref[...] = jnp.exp(i_row * lg_b)
    cd_f = jnp.exp(c * log_gamma(e_f, (1, dk)))
    cd_b = jnp.exp(c * log_gamma(e_b, (1, dk)))
    t = lax.broadcasted_iota(jnp.int32, (c, c), 0)
    s = lax.broadcasted_iota(jnp.int32, (c, c), 1)
    diff = (t - s).astype(F32)
    intra_ref[...] = jnp.where(t >= s, jnp.exp(jnp.where(t >= s, diff, 0.0) * log_gamma(e_f, (c, c))),
                               jnp.exp(jnp.where(t < s, -diff, 0.0) * log_gamma(e_b, (c, c))))

    def load(ci):
        rows = pl.ds(pl.multiple_of(ci * c, c), c)
        q = q_ref[0, 0, rows, :].astype(F32) * (dk ** -0.5)
        k = k_ref[0, 0, rows, :].astype(F32)
        v = v_ref[0, 0, rows, :]
        return rows, q, k, v

    def inter(q_dec, k_dec, v, st_ref, cd):
        y = jnp.dot(q_dec.astype(BF16), st_ref[...].astype(BF16), preferred_element_type=F32)
        kv = lax.dot_general(k_dec.astype(BF16), v, (((0,), (0,)), ((), ())), preferred_element_type=F32)
        st_ref[...] = cd * st_ref[...] + kv
        return y

    sf_ref[...] = jnp.zeros_like(sf_ref)
    sb_ref[...] = jnp.zeros_like(sb_ref)

    def step(ci, carry):
        rows, q, k, v = load(ci)
        sc = lax.dot_general(q.astype(BF16), k.astype(BF16), (((1,), (1,)), ((), ())),
                             preferred_element_type=F32) * intra_ref[...]
        y = jnp.dot(sc.astype(BF16), v, preferred_element_type=F32)
        yf_ref[rows, :] = y + inter(q * qdf_ref[...], k * kdf_ref[...], v, sf_ref, cd_f)
        rows, q, k, v = load(n - 1 - ci)
        yb_ref[rows, :] = inter(q * qdb_ref[...], k * kdb_ref[...], v, sb_ref, cd_b)
        return carry

    lax.fori_loop(0, n, step, 0, unroll=unroll)

    def finish(ci, carry):
        rows = pl.ds(pl.multiple_of(ci * c, c), c)
        y = yf_ref[rows, :] + yb_ref[rows, :]
        y = y * lax.rsqrt(jnp.mean(y * y, axis=-1, keepdims=True) + EPS)
        gate = gate_ref[0, 0, rows, :].astype(F32)
        o_ref[0, rows, :] = (y * (gate * jax.nn.sigmoid(gate))).astype(o_ref.dtype)
        return carry

    lax.fori_loop(0, n, finish, 0, unroll=unroll)


def _retention(proj, decays, *, unroll=8):
    b, _, s, dk = proj.shape
    c = RET_CHUNK

    def head_spec(offset):
        return pl.BlockSpec((1, 1, s, dk), lambda bi, hi: (bi, offset + hi, 0, 0))

    return pl.pallas_call(
        functools.partial(_retention_kernel, seq=s, unroll=unroll),
        grid=(b, RET_HEADS),
        in_specs=[pl.BlockSpec(memory_space=pltpu.SMEM)] + [head_spec(RET_HEADS * i) for i in range(4)],
        out_specs=pl.BlockSpec((1, s, dk), lambda bi, hi: (bi, 0, hi)),
        out_shape=jax.ShapeDtypeStruct((b, s, RET_HEADS * dk), BF16),
        scratch_shapes=[pltpu.VMEM((s, dk), F32)] * 2 + [pltpu.VMEM((dk, dk), F32)] * 2
        + [pltpu.VMEM((c, dk), F32)] * 4 + [pltpu.VMEM((c, c), F32)],
        compiler_params=_params(("parallel", "parallel")),
        name="retention",
    )(decays, proj, proj, proj, proj)


def _cross_heads(q_ref, kv_ref, q_w):
    scale = HEAD_DIM ** -0.5
    per = q_w // HEAD_DIM
    outs = []
    for hh in range(CROSS_HEADS):
        q = q_ref[0, hh // per][:, (hh % per) * HEAD_DIM:(hh % per + 1) * HEAD_DIM]
        k = kv_ref[0, hh]
        v = kv_ref[0, CROSS_HEADS + hh]
        s = lax.dot_general(q, k, (((1,), (1,)), ((), ())), preferred_element_type=F32) * scale
        m = jnp.max(s, axis=-1, keepdims=True)
        p = jnp.exp(s - m)
        l = jnp.sum(p, axis=-1, keepdims=True)
        o = jnp.dot(p.astype(BF16), v, preferred_element_type=F32) / l
        outs.append(o.astype(BF16))
    return jnp.concatenate(outs, axis=1)


def _out_proj_kernel(x_ref, mix_ref, q_ref, kv_ref, w1_ref, w2_ref, o_ref, *, q_w):
    acc = jnp.dot(mix_ref[0], w1_ref[...], preferred_element_type=F32)
    acc = acc + jnp.dot(_cross_heads(q_ref, kv_ref, q_w), w2_ref[...], preferred_element_type=F32)
    o_ref[0] = x_ref[0] + acc


def _out_proj(x, mix, proj, q_head0, kv, w1, w2, *, tm):
    b, s, d = x.shape
    km = mix.shape[-1]
    q_w = proj.shape[-1]
    nq = CROSS_DIM // q_w
    return pl.pallas_call(
        functools.partial(_out_proj_kernel, q_w=q_w),
        grid=(b, s // tm),
        in_specs=[
            pl.BlockSpec((1, tm, d), lambda bi, i: (bi, i, 0)),
            pl.BlockSpec((1, tm, km), lambda bi, i: (bi, i, 0)),
            pl.BlockSpec((1, nq, tm, q_w), lambda bi, i: (bi, q_head0 // nq, i, 0)),
            pl.BlockSpec((1, 2 * CROSS_HEADS, N_MEM, HEAD_DIM), lambda bi, i: (bi, 0, 0, 0)),
            pl.BlockSpec((km, d), lambda bi, i: (0, 0)),
            pl.BlockSpec((CROSS_DIM, d), lambda bi, i: (0, 0)),
        ],
        out_specs=pl.BlockSpec((1, tm, d), lambda bi, i: (bi, i, 0)),
        out_shape=jax.ShapeDtypeStruct((b, s, d), F32),
        compiler_params=_params(("parallel", "parallel")),
        name="out_proj",
    )(x, mix, proj, kv, w1, w2)


def _ffn_kernel(x_ref, g_ref, gf_ref, wg_ref, wu_ref, wd_ref, o_ref, xn_ref, *, final_norm):
    f = pl.program_id(2)

    @pl.when(f == 0)
    def _():
        x = x_ref[0]
        xn_ref[...] = _rmsnorm_f32(x, g_ref[...]).astype(BF16)
        o_ref[0] = x

    xn = xn_ref[...]
    gate = jnp.dot(xn, wg_ref[...], preferred_element_type=F32)
    up = jnp.dot(xn, wu_ref[...], preferred_element_type=F32)
    act = (gate * jax.nn.sigmoid(gate)) * up
    o_ref[0] += jnp.dot(act.astype(BF16), wd_ref[...], preferred_element_type=F32)

    if final_norm:
        @pl.when(f == pl.num_programs(2) - 1)
        def _():
            o_ref[0] = _rmsnorm_f32(o_ref[0], gf_ref[...])


def _ffn(x, g, g_final, w_gate_up, w_down, *, final_norm, tm, tf=TN):
    b, s, d = x.shape
    nf = w_down.shape[0] // tf
    return pl.pallas_call(
        functools.partial(_ffn_kernel, final_norm=final_norm),
        grid=(b, s // tm, nf),
        in_specs=[
            pl.BlockSpec((1, tm, d), lambda bi, i, f: (bi, i, 0)),
            pl.BlockSpec((1, d), lambda bi, i, f: (0, 0)),
            pl.BlockSpec((1, d), lambda bi, i, f: (0, 0)),
            pl.BlockSpec((d, tf), lambda bi, i, f: (0, f)),
            pl.BlockSpec((d, tf), lambda bi, i, f: (0, nf + f)),
            pl.BlockSpec((tf, d), lambda bi, i, f: (f, 0)),
        ],
        out_specs=pl.BlockSpec((1, tm, d), lambda bi, i, f: (bi, i, 0)),
        out_shape=jax.ShapeDtypeStruct((b, s, d), F32),
        scratch_shapes=[pltpu.VMEM((tm, d), BF16)],
        compiler_params=_params(("parallel", "parallel", "arbitrary")),
        name="ffn",
    )(x, g.reshape(1, d), g_final.reshape(1, d), w_gate_up, w_gate_up, w_down)


def _trunk(x, mem, p):
    for i in range(2):
        kv = _norm_proj(mem, p["norm_mem"][i], p["w_mem_kv"][i], head_w=HEAD_DIM, tm=N_MEM)
        if i == 0:
            projs = _norm_proj(x, p["norm_mix"][i], p["a_w_in"], head_w=HEAD_DIM, tm=TM_PROJ,
                               segments=p["a_segments"], q_steps=A_SLOTS * HEAD_DIM // TN, q_scale=ATT_Q_SCALE)
            mix = _dilated_attention(*projs)
            proj, q_head0 = projs[0], 3 * A_SLOTS
            w1, w2 = p["a_w_out1"], p["a_w_out2"]
        else:
            proj, q_head0 = _norm_proj(x, p["norm_mix"][i], p["b_w_in"], head_w=RET_DK, tm=TM_PROJ_B), 4 * RET_HEADS
            mix = _retention(proj, p["b_decay"])
            w1, w2 = p["b_w_out1"], p["b_w_out2"]
        x = _out_proj(x, mix, proj, q_head0, kv, w1, w2, tm=TM_OUT)
        x = _ffn(x, p["norm_ffn"][i], p["norm_final"], p["w_gate_up"][i], p["w_down"][i],
                 final_norm=(i == 1), tm=TM_FFN)
    return x


def _group_columns(w_in):
    gw = A_SLOTS * HEAD_DIM
    cols, segments = [], []
    for g, (_, dil) in enumerate(DIL_GROUPS):
        group = [w_in[:, (c * N_GROUPS + g) * gw:(c * N_GROUPS + g + 1) * gw] for c in range(3)]
        if g == 0:
            group.append(w_in[:, 3 * N_GROUPS * gw:])
        cols += group
        segments.append((sum(w.shape[1] for w in group) // TN, dil))
    return jnp.concatenate(cols, axis=1), tuple(segments)


def kernel(x_prompt, x_sample, mem_prompt, mem_sample, norm_mix, norm_mem, w_mem_kv, a_w_in, a_w_out,
           b_w_in, b_w_out, b_decay_fwd, b_decay_bwd, norm_ffn, w_gate_up, w_down, norm_final):
    a_mix = A_SLOTS * HEAD_DIM
    b_mix = RET_HEADS * RET_DK
    a_cols, a_segments = _group_columns(a_w_in[0])
    p = {
        "norm_mix": norm_mix, "norm_mem": norm_mem, "norm_ffn": norm_ffn, "norm_final": norm_final,
        "w_mem_kv": w_mem_kv.astype(BF16),
        "a_w_in": a_cols.astype(BF16), "a_segments": a_segments,
        "a_w_out1": a_w_out[0, :a_mix].astype(BF16), "a_w_out2": a_w_out[0, a_mix:].astype(BF16),
        "b_w_in": b_w_in[0].astype(BF16),
        "b_w_out1": b_w_out[0, :b_mix].astype(BF16), "b_w_out2": b_w_out[0, b_mix:].astype(BF16),
        "b_decay": jnp.stack([b_decay_fwd[0], b_decay_bwd[0]]).astype(F32),
        "w_gate_up": w_gate_up.astype(BF16), "w_down": w_down.astype(BF16),
    }
    return _trunk(x_prompt, mem_prompt, p), _trunk(x_sample, mem_sample, p)
```

```python
import functools

import jax
import jax.numpy as jnp
from jax import lax
from jax.experimental import pallas as pl
from jax.experimental.pallas import tpu as pltpu

F32 = jnp.float32
BF16 = jnp.bfloat16

D_MODEL = 2048
HEAD_DIM = 128
DIL_GROUPS = ((128, 1), (512, 4), (2048, 16))
N_GROUPS = len(DIL_GROUPS)
A_SLOTS = D_MODEL // 256
A_HEADS = N_GROUPS * A_SLOTS
RET_HEADS = D_MODEL // 256
RET_DK = 256
RET_CHUNK = 256
CROSS_HEADS = 4
CROSS_DIM = CROSS_HEADS * HEAD_DIM
N_MEM = 256
D_FF = 5632
EPS = 1e-6
NEG_BIG = -1e30
LOG2_E = 1.4426950408889634
ATT_Q_SCALE = HEAD_DIM ** -0.5 * LOG2_E

ATT_BLOCK = 128
ATT_HALF = 64
ATT_KEYS = 256
ATT_BATCH = 8
DIL_RATIO = 4

TM_PROJ = 1024
TM_PROJ_B = 2048
TM_OUT = 512
TM_FFN = 1024
TN = 512

V7X_VMEM_LIMIT = 56 * 1024 * 1024
MXU_COLS = 256


def _params(semantics):
    return pltpu.CompilerParams(dimension_semantics=semantics, vmem_limit_bytes=V7X_VMEM_LIMIT)


def _rmsnorm_f32(x, g):
    ms = jnp.mean(x * x, axis=-1, keepdims=True)
    return (x * lax.rsqrt(ms + EPS)) * g


def _norm_proj_kernel(x_ref, g_ref, w_ref, *rest, segments, head_w, half_w, q_steps, q_scale):
    n_seg = len(segments)
    o_refs, xn_ref, scratch = rest[:n_seg], rest[n_seg], rest[n_seg + 1:]
    j = pl.program_id(2)
    tm, tn = xn_ref.shape[0], w_ref.shape[-1]
    per_half = half_w // head_w

    @pl.when(j == 0)
    def _():
        xn_ref[...] = _rmsnorm_f32(x_ref[0], g_ref[...]).astype(BF16)

    def project(half):
        return jnp.dot(xn_ref[...], w_ref[:, half * half_w:(half + 1) * half_w], preferred_element_type=F32)

    def deinterleave(o_ref, h, dil):
        stage_ref = scratch[0]
        if dil <= DIL_RATIO:
            for r in range(dil):
                o_ref[0, h, r] = stage_ref[h, pl.ds(r, tm // dil, stride=dil), :].astype(o_ref.dtype)
            return
        stage2_ref = scratch[1]
        outer = dil // DIL_RATIO
        for r1 in range(DIL_RATIO):
            stage2_ref[h, r1] = stage_ref[h, pl.ds(r1, tm // DIL_RATIO, stride=DIL_RATIO), :]
        for r1 in range(DIL_RATIO):
            for k in range(outer):
                o_ref[0, h, k * DIL_RATIO + r1] = (
                    stage2_ref[h, r1, pl.ds(k, tm // dil, stride=outer), :].astype(o_ref.dtype))

    start = 0
    for o_ref, (steps, dil) in zip(o_refs, segments):
        def segment(o_ref=o_ref, dil=dil, start=start):
            for half in range(tn // half_w):
                res = project(half)
                if q_steps:
                    res = res * jnp.where(j - start < q_steps, q_scale, 1.0)
                for hh in range(per_half):
                    h = half * per_half + hh
                    cols = res[:, hh * head_w:(hh + 1) * head_w]
                    if dil == 1:
                        o_ref[0, h] = cols.astype(o_ref.dtype)
                    else:
                        scratch[0][h] = cols
                if dil > 1:
                    for hh in range(per_half):
                        deinterleave(o_ref, half * per_half + hh, dil)

        if n_seg == 1:
            segment()
        else:
            pl.when((j >= start) & (j < start + steps))(segment)
        start += steps


def _norm_proj(x, g, w, *, head_w, tm, tn=TN, segments=None, q_steps=0, q_scale=1.0):
    b, s, d = x.shape
    nt = w.shape[1] // tn
    hps = tn // head_w
    segments = segments or ((nt, 1),)
    max_dil = max(dil for _, dil in segments)
    scratch = [pltpu.VMEM((tm, d), BF16)]
    if max_dil > 1:
        scratch.append(pltpu.VMEM((hps, tm, head_w), F32))
    if max_dil > DIL_RATIO:
        scratch.append(pltpu.VMEM((hps, DIL_RATIO, tm // DIL_RATIO, head_w), F32))
    out_specs, out_shapes, start = [], [], 0
    for steps, dil in segments:
        def held(j, start=start, steps=steps):
            return jnp.clip(j - start, 0, steps - 1)
        if dil == 1:
            out_specs.append(pl.BlockSpec((1, hps, tm, head_w), lambda bi, i, j, held=held: (bi, held(j), i, 0)))
            out_shapes.append(jax.ShapeDtypeStruct((b, steps * hps, s, head_w), BF16))
        else:
            out_specs.append(pl.BlockSpec((1, hps, dil, tm // dil, head_w),
                                          lambda bi, i, j, held=held: (bi, held(j), 0, i, 0)))
            out_shapes.append(jax.ShapeDtypeStruct((b, steps * hps, dil, s // dil, head_w), BF16))
        start += steps
    kern = functools.partial(_norm_proj_kernel, segments=tuple(segments), head_w=head_w,
                             half_w=max(head_w, MXU_COLS), q_steps=q_steps, q_scale=q_scale)
    outs = pl.pallas_call(
        kern,
        grid=(b, s // tm, nt),
        in_specs=[
            pl.BlockSpec((1, tm, d), lambda bi, i, j: (bi, i, 0)),
            pl.BlockSpec((1, d), lambda bi, i, j: (0, 0)),
            pl.BlockSpec((d, tn), lambda bi, i, j: (0, j)),
        ],
        out_specs=out_specs,
        out_shape=out_shapes,
        scratch_shapes=scratch,
        compiler_params=_params(("parallel", "parallel", "arbitrary")),
        name="norm_proj",
    )(x, g.reshape(1, d), w)
    return outs if len(outs) > 1 else outs[0]


def _dil_attn_kernel(q0, k0, v0, q1, k1, v1, q2, k2, v2, o_ref,
                     m1_ref, l1_ref, a1_ref, m0_ref, l0_ref, a0_ref, bias0_ref, bias1_ref, bias2_ref,
                     *, seq, unroll):
    h = pl.program_id(1)
    groups = ((q0, k0, v0, bias0_ref), (q1, k1, v1, bias1_ref), (q2, k2, v2, bias2_ref))
    stat = (ATT_BLOCK, HEAD_DIM)

    def build_bias(g):
        bias_ref = groups[g][3]
        dil = DIL_GROUPS[g][1]
        kw = bias_ref.shape[-1]
        jf = (h + (g * A_SLOTS + 1)).astype(F32)
        slope = jnp.exp2(jnp.full((ATT_BLOCK, kw), -8.0 / A_HEADS, F32) * jf)
        row = lax.broadcasted_iota(jnp.int32, (ATT_BLOCK, kw), 0)
        col = lax.broadcasted_iota(jnp.int32, (ATT_BLOCK, kw), 1)
        for t, delta in enumerate((0, -ATT_HALF, ATT_BLOCK - kw)):
            arel = jnp.abs(col - row + delta)
            bias_ref[t] = jnp.where(arel <= ATT_HALF, (-slope * (dil * arel).astype(F32)) * LOG2_E, NEG_BIG)

    def scores(g, q, k, blk, nblk):
        bias_ref = groups[g][3]
        table = jnp.where(blk == 0, 0, jnp.where(blk == nblk - 1, 2, 1))
        return lax.dot_general(q, k, (((1,), (1,)), ((), ())), preferred_element_type=F32) + bias_ref[table]

    def accumulate(s, v, state):
        m_new = jnp.broadcast_to(jnp.max(s, axis=-1, keepdims=True), stat)
        if state is not None:
            m_old, l_old, acc_old = state
            m_new = jnp.maximum(m_old, m_new)
            alpha = jnp.exp2(m_old - m_new)
        slabs = [s[:, c:c + HEAD_DIM] - m_new for c in range(0, s.shape[1], HEAD_DIM)]
        p = jnp.exp2(slabs[0] if len(slabs) == 1 else jnp.concatenate(slabs, axis=1))
        l_new = jnp.broadcast_to(jnp.sum(p, axis=-1, keepdims=True), stat)
        acc_new = jnp.dot(p.astype(BF16), v, preferred_element_type=F32)
        if state is not None:
            l_new = alpha * l_old + l_new
            acc_new = alpha * acc_old + acc_new
        return m_new, l_new, acc_new

    def run_group(g):
        q_ref, k_ref, v_ref, bias_ref = groups[g]
        dil = DIL_GROUPS[g][1]
        length = seq // dil
        kw = bias_ref.shape[-1]
        nblk = length // ATT_BLOCK
        shift = nblk.bit_length() - 1

        def score_block(it):
            r = it >> shift
            blk = it & (nblk - 1)
            i0 = pl.multiple_of(blk * ATT_BLOCK, ATT_BLOCK)
            ws = pl.multiple_of(jnp.clip(i0 - ATT_HALF, 0, length - kw), ATT_HALF)
            if g == 0:
                q = q_ref[0, 0, pl.ds(i0, ATT_BLOCK), :]
                k = k_ref[0, 0, pl.ds(ws, kw), :]
                v = v_ref[0, 0, pl.ds(ws, kw), :]
            else:
                q = q_ref[0, 0, r, pl.ds(i0, ATT_BLOCK), :]
                k = k_ref[0, 0, r, pl.ds(ws, kw), :]
                v = v_ref[0, 0, r, pl.ds(ws, kw), :]
            return r, i0, scores(g, q, k, blk, nblk), v

        def finish_block(r, i0, s, v):
            if g == 2:
                m, l, acc = accumulate(s, v, None)
                rows = pl.ds(i0 * DIL_RATIO + (r >> 2), ATT_BLOCK, stride=DIL_RATIO)
                m1_ref[r & 3, rows, :] = m
                l1_ref[r & 3, rows, :] = l
                a1_ref[r & 3, rows, :] = acc
            elif g == 1:
                rows = pl.ds(i0, ATT_BLOCK)
                m, l, acc = accumulate(s, v, (m1_ref[r, rows, :], l1_ref[r, rows, :], a1_ref[r, rows, :]))
                rows = pl.ds(i0 * DIL_RATIO + r, ATT_BLOCK, stride=DIL_RATIO)
                m0_ref[rows, :] = m
                l0_ref[rows, :] = l
                a0_ref[rows, :] = acc
            else:
                rows = pl.ds(i0, ATT_BLOCK)
                _, l, acc = accumulate(s, v, (m0_ref[rows, :], l0_ref[rows, :], a0_ref[rows, :]))
                o_ref[0, rows, :] = (acc / l).astype(o_ref.dtype)

        total = dil * nblk
        batch = min(ATT_BATCH, total)

        def body(bi, carry):
            blocks = [score_block(bi * batch + i) for i in range(batch)]
            for blk_args in blocks:
                finish_block(*blk_args)
            return carry

        lax.fori_loop(0, total // batch, body, 0, unroll=max(1, min(unroll, total) // batch))

    for g in range(N_GROUPS):
        build_bias(g)
    run_group(2)
    run_group(1)
    run_group(0)


def _dilated_attention(p0, p1, p2, *, unroll=32):
    b, _, s, hd = p0.shape
    args, specs = [], []
    for c in range(3):
        args.append(p0)
        specs.append(pl.BlockSpec((1, 1, s, hd), lambda bi, hi, c=c: (bi, c * A_SLOTS + hi, 0, 0)))
    for pg in (p1, p2):
        dil, length = pg.shape[2], pg.shape[3]
        for c in range(3):
            args.append(pg)
            specs.append(pl.BlockSpec((1, 1, dil, length, hd), lambda bi, hi, c=c: (bi, c * A_SLOTS + hi, 0, 0, 0)))
    l1 = s // DIL_GROUPS[1][1]
    bias = [pltpu.VMEM((3, ATT_BLOCK, min(ATT_KEYS, s // dil)), F32) for _, dil in DIL_GROUPS]
    return pl.pallas_call(
        functools.partial(_dil_attn_kernel, seq=s, unroll=unroll),
        grid=(b, A_SLOTS),
        in_specs=specs,
        out_specs=pl.BlockSpec((1, s, hd), lambda bi, hi: (bi, 0, hi)),
        out_shape=jax.ShapeDtypeStruct((b, s, A_SLOTS * hd), BF16),
        scratch_shapes=[pltpu.VMEM((DIL_RATIO, l1, hd), F32)] * 3 + [pltpu.VMEM((s, hd), F32)] * 3 + bias,
        compiler_params=_params(("parallel", "parallel")),
        name="dilated_attention",
    )(*args)


def _retention_kernel(dec_ref, q_ref, k_ref, v_ref, gate_ref, o_ref,
                      yf_ref, yb_ref, sf_ref, sb_ref, qdf_ref, kdf_ref, qdb_ref, kdb_ref, intra_ref,
                      *, seq, unroll):
    h = pl.program_id(1)
    c = RET_CHUNK
    n = seq // c
    dk = RET_DK

    def log_gamma(e, shape):
        return jnp.log1p(-jnp.exp2(-jnp.full(shape, e, F32)))

    e_f = dec_ref[0, h]
    e_b = dec_ref[1, h]
    lg_f = log_gamma(e_f, (c, dk))
    lg_b = log_gamma(e_b, (c, dk))
    i_row = lax.broadcasted_iota(jnp.int32, (c, dk), 0).astype(F32)
    qdf_ref[...] = jnp.exp((i_row + 1.0) * lg_f)
    kdf_ref[...] = jnp.exp((c - 1.0 - i_row) * lg_f)
    qdb_ref[...] = jnp.exp((c - i_row) * lg_b)
    kdb_ref[...] = jnp.exp(i_row * lg_b)
    cd_f = jnp.exp(c * log_gamma(e_f, (1, dk)))
    cd_b = jnp.exp(c * log_gamma(e_b, (1, dk)))
    t = lax.broadcasted_iota(jnp.int32, (c, c), 0)
    s = lax.broadcasted_iota(jnp.int32, (c, c), 1)
    diff = (t - s).astype(F32)
    intra_ref[...] = jnp.where(t >= s, jnp.exp(jnp.where(t >= s, diff, 0.0) * log_gamma(e_f, (c, c))),
                               jnp.exp(jnp.where(t < s, -diff, 0.0) * log_gamma(e_b, (c, c))))

    def load(ci):
        rows = pl.ds(pl.multiple_of(ci * c, c), c)
        q = q_ref[0, 0, rows, :].astype(F32) * (dk ** -0.5)
        k = k_ref[0, 0, rows, :].astype(F32)
        v = v_ref[0, 0, rows, :]
        return rows, q, k, v

    def inter(q_dec, k_dec, v, st_ref, cd):
        y = jnp.dot(q_dec.astype(BF16), st_ref[...].astype(BF16), preferred_element_type=F32)
        kv = lax.dot_general(k_dec.astype(BF16), v, (((0,), (0,)), ((), ())), preferred_element_type=F32)
        st_ref[...] = cd * st_ref[...] + kv
        return y

    sf_ref[...] = jnp.zeros_like(sf_ref)
    sb_ref[...] = jnp.zeros_like(sb_ref)

    def finish(rows, y):
        y = y * lax.rsqrt(jnp.mean(y * y, axis=-1, keepdims=True) + EPS)
        gate = gate_ref[0, 0, rows, :].astype(F32)
        o_ref[0, rows, :] = (y * (gate * jax.nn.sigmoid(gate))).astype(o_ref.dtype)

    half = n // 2

    def step(ci, carry, *, second_half):
        cj = n - 1 - ci
        rows, q, k, v = load(ci)
        sc = lax.dot_general(q.astype(BF16), k.astype(BF16), (((1,), (1,)), ((), ())),
                             preferred_element_type=F32) * intra_ref[...]
        y_f = jnp.dot(sc.astype(BF16), v, preferred_element_type=F32)
        y_f = y_f + inter(q * qdf_ref[...], k * kdf_ref[...], v, sf_ref, cd_f)
        if second_half:
            finish(rows, y_f + yb_ref[pl.ds(pl.multiple_of((ci - half) * c, c), c), :])
        else:
            yf_ref[rows, :] = y_f
        rows, q, k, v = load(cj)
        y_b = inter(q * qdb_ref[...], k * kdb_ref[...], v, sb_ref, cd_b)
        if second_half:
            finish(rows, yf_ref[rows, :] + y_b)
        else:
            yb_ref[pl.ds(pl.multiple_of((cj - half) * c, c), c), :] = y_b
        return carry

    lax.fori_loop(0, half, functools.partial(step, second_half=False), 0, unroll=min(unroll, half))
    lax.fori_loop(half, n, functools.partial(step, second_half=True), 0, unroll=min(unroll, half))


def _retention(proj, decays, *, unroll=8):
    b, _, s, dk = proj.shape
    c = RET_CHUNK

    def head_spec(offset):
        return pl.BlockSpec((1, 1, s, dk), lambda bi, hi: (bi, offset + hi, 0, 0))

    return pl.pallas_call(
        functools.partial(_retention_kernel, seq=s, unroll=unroll),
        grid=(b, RET_HEADS),
        in_specs=[pl.BlockSpec(memory_space=pltpu.SMEM)] + [head_spec(RET_HEADS * i) for i in range(4)],
        out_specs=pl.BlockSpec((1, s, dk), lambda bi, hi: (bi, 0, hi)),
        out_shape=jax.ShapeDtypeStruct((b, s, RET_HEADS * dk), BF16),
        scratch_shapes=[pltpu.VMEM((s // 2, dk), F32)] * 2 + [pltpu.VMEM((dk, dk), F32)] * 2
        + [pltpu.VMEM((c, dk), F32)] * 4 + [pltpu.VMEM((c, c), F32)],
        compiler_params=_params(("parallel", "parallel")),
        name="retention",
    )(decays, proj, proj, proj, proj)


def _cross_heads(q_ref, kv_ref, q_w):
    scale = HEAD_DIM ** -0.5
    per = q_w // HEAD_DIM
    outs = []
    for hh in range(CROSS_HEADS):
        q = q_ref[0, hh // per][:, (hh % per) * HEAD_DIM:(hh % per + 1) * HEAD_DIM]
        k = kv_ref[0, hh]
        v = kv_ref[0, CROSS_HEADS + hh]
        s = lax.dot_general(q, k, (((1,), (1,)), ((), ())), preferred_element_type=F32) * scale
        m = jnp.max(s, axis=-1, keepdims=True)
        p = jnp.exp(s - m)
        l = jnp.sum(p, axis=-1, keepdims=True)
        o = jnp.dot(p.astype(BF16), v, preferred_element_type=F32) / l
        outs.append(o.astype(BF16))
    return jnp.concatenate(outs, axis=1)


def _out_proj_kernel(x_ref, mix_ref, q_ref, kv_ref, w1_ref, w2_ref, o_ref, *, q_w):
    acc = jnp.dot(mix_ref[0], w1_ref[...], preferred_element_type=F32)
    acc = acc + jnp.dot(_cross_heads(q_ref, kv_ref, q_w), w2_ref[...], preferred_element_type=F32)
    o_ref[0] = x_ref[0] + acc


def _out_proj(x, mix, proj, q_head0, kv, w1, w2, *, tm):
    b, s, d = x.shape
    km = mix.shape[-1]
    q_w = proj.shape[-1]
    nq = CROSS_DIM // q_w
    return pl.pallas_call(
        functools.partial(_out_proj_kernel, q_w=q_w),
        grid=(b, s // tm),
        in_specs=[
            pl.BlockSpec((1, tm, d), lambda bi, i: (bi, i, 0)),
            pl.BlockSpec((1, tm, km), lambda bi, i: (bi, i, 0)),
            pl.BlockSpec((1, nq, tm, q_w), lambda bi, i: (bi, q_head0 // nq, i, 0)),
            pl.BlockSpec((1, 2 * CROSS_HEADS, N_MEM, HEAD_DIM), lambda bi, i: (0, 0, bi, 0)),
            pl.BlockSpec((km, d), lambda bi, i: (0, 0)),
            pl.BlockSpec((CROSS_DIM, d), lambda bi, i: (0, 0)),
        ],
        out_specs=pl.BlockSpec((1, tm, d), lambda bi, i: (bi, i, 0)),
        out_shape=jax.ShapeDtypeStruct((b, s, d), F32),
        compiler_params=_params(("parallel", "parallel")),
        name="out_proj",
    )(x, mix, proj, kv, w1, w2)


def _ffn_kernel(x_ref, g_ref, gf_ref, wg_ref, wu_ref, wd_ref, o_ref, xn_ref, *, final_norm):
    f = pl.program_id(2)

    @pl.when(f == 0)
    def _():
        x = x_ref[0]
        xn_ref[...] = _rmsnorm_f32(x, g_ref[...]).astype(BF16)
        o_ref[0] = x

    xn = xn_ref[...]
    gate = jnp.dot(xn, wg_ref[...], preferred_element_type=F32)
    up = jnp.dot(xn, wu_ref[...], preferred_element_type=F32)
    act = (gate * jax.nn.sigmoid(gate)) * up
    o_ref[0] += jnp.dot(act.astype(BF16), wd_ref[...], preferred_element_type=F32)

    if final_norm:
        @pl.when(f == pl.num_programs(2) - 1)
        def _():
            o_ref[0] = _rmsnorm_f32(o_ref[0], gf_ref[...])


def _ffn(x, g, g_final, w_gate_up, w_down, *, final_norm, tm, tf=TN):
    b, s, d = x.shape
    nf = w_down.shape[0] // tf
    return pl.pallas_call(
        functools.partial(_ffn_kernel, final_norm=final_norm),
        grid=(b, s // tm, nf),
        in_specs=[
            pl.BlockSpec((1, tm, d), lambda bi, i, f: (bi, i, 0)),
            pl.BlockSpec((1, d), lambda bi, i, f: (0, 0)),
            pl.BlockSpec((1, d), lambda bi, i, f: (0, 0)),
            pl.BlockSpec((d, tf), lambda bi, i, f: (0, f)),
            pl.BlockSpec((d, tf), lambda bi, i, f: (0, nf + f)),
            pl.BlockSpec((tf, d), lambda bi, i, f: (f, 0)),
        ],
        out_specs=pl.BlockSpec((1, tm, d), lambda bi, i, f: (bi, i, 0)),
        out_shape=jax.ShapeDtypeStruct((b, s, d), F32),
        scratch_shapes=[pltpu.VMEM((tm, d), BF16)],
        compiler_params=_params(("parallel", "parallel", "arbitrary")),
        name="ffn",
    )(x, g.reshape(1, d), g_final.reshape(1, d), w_gate_up, w_gate_up, w_down)


def _trunk(x, mem, p):
    for i in range(2):
        kv = _norm_proj(mem.reshape(1, -1, mem.shape[-1]), p["norm_mem"][i], p["w_mem_kv"][i],
                        head_w=HEAD_DIM, tm=mem.shape[0] * N_MEM)
        if i == 0:
            projs = _norm_proj(x, p["norm_mix"][i], p["a_w_in"], head_w=HEAD_DIM, tm=TM_PROJ,
                               segments=p["a_segments"], q_steps=A_SLOTS * HEAD_DIM // TN, q_scale=ATT_Q_SCALE)
            mix = _dilated_attention(*projs)
            proj, q_head0 = projs[0], 3 * A_SLOTS
            w1, w2 = p["a_w_out1"], p["a_w_out2"]
        else:
            proj, q_head0 = _norm_proj(x, p["norm_mix"][i], p["b_w_in"], head_w=RET_DK, tm=TM_PROJ_B), 4 * RET_HEADS
            mix = _retention(proj, p["b_decay"])
            w1, w2 = p["b_w_out1"], p["b_w_out2"]
        x = _out_proj(x, mix, proj, q_head0, kv, w1, w2, tm=TM_OUT)
        x = _ffn(x, p["norm_ffn"][i], p["norm_final"], p["w_gate_up"][i], p["w_down"][i],
                 final_norm=(i == 1), tm=TM_FFN)
    return x


def _group_columns(w_in):
    gw = A_SLOTS * HEAD_DIM
    cols, segments = [], []
    for g, (_, dil) in enumerate(DIL_GROUPS):
        group = [w_in[:, (c * N_GROUPS + g) * gw:(c * N_GROUPS + g + 1) * gw] for c in range(3)]
        if g == 0:
            group.append(w_in[:, 3 * N_GROUPS * gw:])
        cols += group
        segments.append((sum(w.shape[1] for w in group) // TN, dil))
    return jnp.concatenate(cols, axis=1), tuple(segments)


def kernel(x_prompt, x_sample, mem_prompt, mem_sample, norm_mix, norm_mem, w_mem_kv, a_w_in, a_w_out,
           b_w_in, b_w_out, b_decay_fwd, b_decay_bwd, norm_ffn, w_gate_up, w_down, norm_final):
    a_mix = A_SLOTS * HEAD_DIM
    b_mix = RET_HEADS * RET_DK
    a_cols, a_segments = _group_columns(a_w_in[0])
    p = {
        "norm_mix": norm_mix, "norm_mem": norm_mem, "norm_ffn": norm_ffn, "norm_final": norm_final,
        "w_mem_kv": w_mem_kv.astype(BF16),
        "a_w_in": a_cols.astype(BF16), "a_segments": a_segments,
        "a_w_out1": a_w_out[0, :a_mix].astype(BF16), "a_w_out2": a_w_out[0, a_mix:].astype(BF16),
        "b_w_in": b_w_in[0].astype(BF16),
        "b_w_out1": b_w_out[0, :b_mix].astype(BF16), "b_w_out2": b_w_out[0, b_mix:].astype(BF16),
        "b_decay": jnp.stack([b_decay_fwd[0], b_decay_bwd[0]]).astype(F32),
        "w_gate_up": w_gate_up.astype(BF16), "w_down": w_down.astype(BF16),
    }
    return _trunk(x_prompt, mem_prompt, p), _trunk(x_sample, mem_sample, p)
```

```python
import functools

import jax
import jax.numpy as jnp
from jax import lax
from jax.experimental import pallas as pl
from jax.experimental.pallas import tpu as pltpu

F32 = jnp.float32
BF16 = jnp.bfloat16

D_MODEL = 2048
HEAD_DIM = 128
DIL_GROUPS = ((128, 1), (512, 4), (2048, 16))
N_GROUPS = len(DIL_GROUPS)
A_SLOTS = D_MODEL // 256
A_HEADS = N_GROUPS * A_SLOTS
RET_HEADS = D_MODEL // 256
RET_DK = 256
RET_CHUNK = 256
CROSS_HEADS = 4
CROSS_DIM = CROSS_HEADS * HEAD_DIM
N_MEM = 256
D_FF = 5632
EPS = 1e-6
NEG_BIG = -1e30
LOG2_E = 1.4426950408889634
ATT_Q_SCALE = HEAD_DIM ** -0.5 * LOG2_E

ATT_BLOCK = 128
ATT_HALF = 64
ATT_KEYS = 256
ATT_BATCH = 8
DIL_RATIO = 4

TM_PROJ = 1024
TM_PROJ_B = 2048
TM_OUT = 512
TM_FFN = 1024
TN = 512
TN_A = 1024

V7X_VMEM_LIMIT = 56 * 1024 * 1024
MXU_COLS = 256


def _params(semantics):
    return pltpu.CompilerParams(dimension_semantics=semantics, vmem_limit_bytes=V7X_VMEM_LIMIT)


def _rmsnorm_f32(x, g):
    ms = jnp.mean(x * x, axis=-1, keepdims=True)
    return (x * lax.rsqrt(ms + EPS)) * g


def _norm_proj_kernel(x_ref, g_ref, w_ref, *rest, segments, head_w, half_w, q_steps, q_scale):
    n_seg = len(segments)
    o_refs, xn_ref, scratch = rest[:n_seg], rest[n_seg], rest[n_seg + 1:]
    j = pl.program_id(2)
    tm, tn = xn_ref.shape[0], w_ref.shape[-1]
    per_half = half_w // head_w

    @pl.when(j == 0)
    def _():
        xn_ref[...] = _rmsnorm_f32(x_ref[0], g_ref[...]).astype(BF16)

    def project(half):
        return jnp.dot(xn_ref[...], w_ref[:, half * half_w:(half + 1) * half_w], preferred_element_type=F32)

    def deinterleave(o_ref, h, dil):
        stage_ref = scratch[0]
        if dil <= DIL_RATIO:
            for r in range(dil):
                o_ref[0, h, r] = stage_ref[h, pl.ds(r, tm // dil, stride=dil), :].astype(o_ref.dtype)
            return
        stage2_ref = scratch[1]
        outer = dil // DIL_RATIO
        for r1 in range(DIL_RATIO):
            stage2_ref[h, r1] = stage_ref[h, pl.ds(r1, tm // DIL_RATIO, stride=DIL_RATIO), :]
        for r1 in range(DIL_RATIO):
            for k in range(outer):
                o_ref[0, h, k * DIL_RATIO + r1] = (
                    stage2_ref[h, r1, pl.ds(k, tm // dil, stride=outer), :].astype(o_ref.dtype))

    start = 0
    for o_ref, (steps, dil) in zip(o_refs, segments):
        def segment(o_ref=o_ref, dil=dil, start=start):
            for half in range(tn // half_w):
                res = project(half)
                if q_steps:
                    res = res * jnp.where(j - start < q_steps, q_scale, 1.0)
                for hh in range(per_half):
                    h = half * per_half + hh
                    cols = res[:, hh * head_w:(hh + 1) * head_w]
                    if dil == 1:
                        o_ref[0, h] = cols.astype(o_ref.dtype)
                    else:
                        scratch[0][h] = cols
                if dil > 1:
                    for hh in range(per_half):
                        deinterleave(o_ref, half * per_half + hh, dil)

        if n_seg == 1:
            segment()
        else:
            pl.when((j >= start) & (j < start + steps))(segment)
        start += steps


def _norm_proj(x, g, w, *, head_w, tm, tn=TN, segments=None, q_steps=0, q_scale=1.0):
    b, s, d = x.shape
    nt = w.shape[1] // tn
    hps = tn // head_w
    segments = segments or ((nt, 1),)
    max_dil = max(dil for _, dil in segments)
    scratch = [pltpu.VMEM((tm, d), BF16)]
    if max_dil > 1:
        scratch.append(pltpu.VMEM((hps, tm, head_w), F32))
    if max_dil > DIL_RATIO:
        scratch.append(pltpu.VMEM((hps, DIL_RATIO, tm // DIL_RATIO, head_w), F32))
    out_specs, out_shapes, start = [], [], 0
    for steps, dil in segments:
        def held(j, start=start, steps=steps):
            return jnp.clip(j - start, 0, steps - 1)
        if dil == 1:
            out_specs.append(pl.BlockSpec((1, hps, tm, head_w), lambda bi, i, j, held=held: (bi, held(j), i, 0)))
            out_shapes.append(jax.ShapeDtypeStruct((b, steps * hps, s, head_w), BF16))
        else:
            out_specs.append(pl.BlockSpec((1, hps, dil, tm // dil, head_w),
                                          lambda bi, i, j, held=held: (bi, held(j), 0, i, 0)))
            out_shapes.append(jax.ShapeDtypeStruct((b, steps * hps, dil, s // dil, head_w), BF16))
        start += steps
    kern = functools.partial(_norm_proj_kernel, segments=tuple(segments), head_w=head_w,
                             half_w=max(head_w, MXU_COLS), q_steps=q_steps, q_scale=q_scale)
    outs = pl.pallas_call(
        kern,
        grid=(b, s // tm, nt),
        in_specs=[
            pl.BlockSpec((1, tm, d), lambda bi, i, j: (bi, i, 0)),
            pl.BlockSpec((1, d), lambda bi, i, j: (0, 0)),
            pl.BlockSpec((d, tn), lambda bi, i, j: (0, j)),
        ],
        out_specs=out_specs,
        out_shape=out_shapes,
        scratch_shapes=scratch,
        compiler_params=_params(("parallel", "parallel", "arbitrary")),
        name="norm_proj",
    )(x, g.reshape(1, d), w)
    return outs if len(outs) > 1 else outs[0]


def _dil_attn_kernel(q0, k0, v0, q1, k1, v1, q2, k2, v2, o_ref,
                     m1_ref, l1_ref, a1_ref, m0_ref, l0_ref, a0_ref, bias0_ref, bias1_ref, bias2_ref,
                     *, seq, unroll):
    h = pl.program_id(1)
    groups = ((q0, k0, v0, bias0_ref), (q1, k1, v1, bias1_ref), (q2, k2, v2, bias2_ref))
    stat = (ATT_BLOCK, HEAD_DIM)

    def build_bias(g):
        bias_ref = groups[g][3]
        dil = DIL_GROUPS[g][1]
        kw = bias_ref.shape[-1]
        jf = (h + (g * A_SLOTS + 1)).astype(F32)
        slope = jnp.exp2(jnp.full((ATT_BLOCK, kw), -8.0 / A_HEADS, F32) * jf)
        row = lax.broadcasted_iota(jnp.int32, (ATT_BLOCK, kw), 0)
        col = lax.broadcasted_iota(jnp.int32, (ATT_BLOCK, kw), 1)
        for t, delta in enumerate((0, -ATT_HALF, ATT_BLOCK - kw)):
            arel = jnp.abs(col - row + delta)
            bias_ref[t] = jnp.where(arel <= ATT_HALF, (-slope * (dil * arel).astype(F32)) * LOG2_E, NEG_BIG)

    def scores(g, q, k, blk, nblk):
        bias_ref = groups[g][3]
        table = jnp.where(blk == 0, 0, jnp.where(blk == nblk - 1, 2, 1))
        return lax.dot_general(q, k, (((1,), (1,)), ((), ())), preferred_element_type=F32) + bias_ref[table]

    def accumulate(s, v, state):
        m_new = jnp.broadcast_to(jnp.max(s, axis=-1, keepdims=True), stat)
        if state is not None:
            m_old, l_old, acc_old = state
            m_new = jnp.maximum(m_old, m_new)
            alpha = jnp.exp2(m_old - m_new)
        slabs = [s[:, c:c + HEAD_DIM] - m_new for c in range(0, s.shape[1], HEAD_DIM)]
        p = jnp.exp2(slabs[0] if len(slabs) == 1 else jnp.concatenate(slabs, axis=1))
        l_new = jnp.broadcast_to(jnp.sum(p, axis=-1, keepdims=True), stat)
        acc_new = jnp.dot(p.astype(BF16), v, preferred_element_type=F32)
        if state is not None:
            l_new = alpha * l_old + l_new
            acc_new = alpha * acc_old + acc_new
        return m_new, l_new, acc_new

    def run_group(g):
        q_ref, k_ref, v_ref, bias_ref = groups[g]
        dil = DIL_GROUPS[g][1]
        length = seq // dil
        kw = bias_ref.shape[-1]
        nblk = length // ATT_BLOCK
        shift = nblk.bit_length() - 1

        def score_block(it):
            r = it >> shift
            blk = it & (nblk - 1)
            i0 = pl.multiple_of(blk * ATT_BLOCK, ATT_BLOCK)
            ws = pl.multiple_of(jnp.clip(i0 - ATT_HALF, 0, length - kw), ATT_HALF)
            if g == 0:
                q = q_ref[0, 0, pl.ds(i0, ATT_BLOCK), :]
                k = k_ref[0, 0, pl.ds(ws, kw), :]
                v = v_ref[0, 0, pl.ds(ws, kw), :]
            else:
                q = q_ref[0, 0, r, pl.ds(i0, ATT_BLOCK), :]
                k = k_ref[0, 0, r, pl.ds(ws, kw), :]
                v = v_ref[0, 0, r, pl.ds(ws, kw), :]
            return r, i0, scores(g, q, k, blk, nblk), v

        def finish_block(r, i0, s, v):
            if g == 2:
                m, l, acc = accumulate(s, v, None)
                rows = pl.ds(i0 * DIL_RATIO + (r >> 2), ATT_BLOCK, stride=DIL_RATIO)
                m1_ref[r & 3, rows, :] = m
                l1_ref[r & 3, rows, :] = l
                a1_ref[r & 3, rows, :] = acc
            elif g == 1:
                rows = pl.ds(i0, ATT_BLOCK)
                m, l, acc = accumulate(s, v, (m1_ref[r, rows, :], l1_ref[r, rows, :], a1_ref[r, rows, :]))
                rows = pl.ds(i0 * DIL_RATIO + r, ATT_BLOCK, stride=DIL_RATIO)
                m0_ref[rows, :] = m
                l0_ref[rows, :] = l
                a0_ref[rows, :] = acc
            else:
                rows = pl.ds(i0, ATT_BLOCK)
                _, l, acc = accumulate(s, v, (m0_ref[rows, :], l0_ref[rows, :], a0_ref[rows, :]))
                o_ref[0, rows, :] = (acc / l).astype(o_ref.dtype)

        total = dil * nblk
        batch = min(ATT_BATCH, total)

        def body(bi, carry):
            blocks = [score_block(bi * batch + i) for i in range(batch)]
            for blk_args in blocks:
                finish_block(*blk_args)
            return carry

        lax.fori_loop(0, total // batch, body, 0, unroll=max(1, min(unroll, total) // batch))

    for g in range(N_GROUPS):
        build_bias(g)
    run_group(2)
    run_group(1)
    run_group(0)


def _dilated_attention(p0, p1, p2, *, unroll=32):
    b, _, s, hd = p0.shape
    args, specs = [], []
    for c in range(3):
        args.append(p0)
        specs.append(pl.BlockSpec((1, 1, s, hd), lambda bi, hi, c=c: (bi, c * A_SLOTS + hi, 0, 0)))
    for pg in (p1, p2):
        dil, length = pg.shape[2], pg.shape[3]
        for c in range(3):
            args.append(pg)
            specs.append(pl.BlockSpec((1, 1, dil, length, hd), lambda bi, hi, c=c: (bi, c * A_SLOTS + hi, 0, 0, 0)))
    l1 = s // DIL_GROUPS[1][1]
    bias = [pltpu.VMEM((3, ATT_BLOCK, min(ATT_KEYS, s // dil)), F32) for _, dil in DIL_GROUPS]
    return pl.pallas_call(
        functools.partial(_dil_attn_kernel, seq=s, unroll=unroll),
        grid=(b, A_SLOTS),
        in_specs=specs,
        out_specs=pl.BlockSpec((1, s, hd), lambda bi, hi: (bi, 0, hi)),
        out_shape=jax.ShapeDtypeStruct((b, s, A_SLOTS * hd), BF16),
        scratch_shapes=[pltpu.VMEM((DIL_RATIO, l1, hd), F32)] * 3 + [pltpu.VMEM((s, hd), F32)] * 3 + bias,
        compiler_params=_params(("parallel", "parallel")),
        name="dilated_attention",
    )(*args)


def _retention_kernel(dec_ref, q_ref, k_ref, v_ref, gate_ref, o_ref,
                      yf_ref, yb_ref, sf_ref, sb_ref, qdf_ref, kdf_ref, qdb_ref, kdb_ref, intra_ref,
                      *, seq, unroll):
    h = pl.program_id(1)
    c = RET_CHUNK
    n = seq // c
    dk = RET_DK

    def log_gamma(e, shape):
        return jnp.log1p(-jnp.exp2(-jnp.full(shape, e, F32)))

    e_f = dec_ref[0, h]
    e_b = dec_ref[1, h]
    lg_f = log_gamma(e_f, (c, dk))
    lg_b = log_gamma(e_b, (c, dk))
    i_row = lax.broadcasted_iota(jnp.int32, (c, dk), 0).astype(F32)
    qdf_ref[...] = jnp.exp((i_row + 1.0) * lg_f)
    kdf_ref[...] = jnp.exp((c - 1.0 - i_row) * lg_f)
    qdb_ref[...] = jnp.exp((c - i_row) * lg_b)
    kdb_ref[...] = jnp.exp(i_row * lg_b)
    cd_f = jnp.exp(c * log_gamma(e_f, (1, dk)))
    cd_b = jnp.exp(c * log_gamma(e_b, (1, dk)))
    t = lax.broadcasted_iota(jnp.int32, (c, c), 0)
    s = lax.broadcasted_iota(jnp.int32, (c, c), 1)
    diff = (t - s).astype(F32)
    intra_ref[...] = jnp.where(t >= s, jnp.exp(jnp.where(t >= s, diff, 0.0) * log_gamma(e_f, (c, c))),
                               jnp.exp(jnp.where(t < s, -diff, 0.0) * log_gamma(e_b, (c, c))))

    def load(ci):
        rows = pl.ds(pl.multiple_of(ci * c, c), c)
        q = q_ref[0, 0, rows, :].astype(F32) * (dk ** -0.5)
        k = k_ref[0, 0, rows, :].astype(F32)
        v = v_ref[0, 0, rows, :]
        return rows, q, k, v

    def inter(q_dec, k_dec, v, st_ref, cd):
        y = jnp.dot(q_dec.astype(BF16), st_ref[...].astype(BF16), preferred_element_type=F32)
        kv = lax.dot_general(k_dec.astype(BF16), v, (((0,), (0,)), ((), ())), preferred_element_type=F32)
        st_ref[...] = cd * st_ref[...] + kv
        return y

    sf_ref[...] = jnp.zeros_like(sf_ref)
    sb_ref[...] = jnp.zeros_like(sb_ref)

    def finish(rows, y):
        y = y * lax.rsqrt(jnp.mean(y * y, axis=-1, keepdims=True) + EPS)
        gate = gate_ref[0, 0, rows, :].astype(F32)
        o_ref[0, rows, :] = (y * (gate * jax.nn.sigmoid(gate))).astype(o_ref.dtype)

    half = n // 2

    def step(ci, carry, *, second_half):
        cj = n - 1 - ci
        rows, q, k, v = load(ci)
        sc = lax.dot_general(q.astype(BF16), k.astype(BF16), (((1,), (1,)), ((), ())),
                             preferred_element_type=F32) * intra_ref[...]
        y_f = jnp.dot(sc.astype(BF16), v, preferred_element_type=F32)
        y_f = y_f + inter(q * qdf_ref[...], k * kdf_ref[...], v, sf_ref, cd_f)
        if second_half:
            finish(rows, y_f + yb_ref[pl.ds(pl.multiple_of((ci - half) * c, c), c), :])
        else:
            yf_ref[rows, :] = y_f
        rows, q, k, v = load(cj)
        y_b = inter(q * qdb_ref[...], k * kdb_ref[...], v, sb_ref, cd_b)
        if second_half:
            finish(rows, yf_ref[rows, :] + y_b)
        else:
            yb_ref[pl.ds(pl.multiple_of((cj - half) * c, c), c), :] = y_b
        return carry

    lax.fori_loop(0, half, functools.partial(step, second_half=False), 0, unroll=min(unroll, half))
    lax.fori_loop(half, n, functools.partial(step, second_half=True), 0, unroll=min(unroll, half))


def _retention(proj, decays, *, unroll=8):
    b, _, s, dk = proj.shape
    c = RET_CHUNK

    def head_spec(offset):
        return pl.BlockSpec((1, 1, s, dk), lambda bi, hi: (bi, offset + hi, 0, 0))

    return pl.pallas_call(
        functools.partial(_retention_kernel, seq=s, unroll=unroll),
        grid=(b, RET_HEADS),
        in_specs=[pl.BlockSpec(memory_space=pltpu.SMEM)] + [head_spec(RET_HEADS * i) for i in range(4)],
        out_specs=pl.BlockSpec((1, s, dk), lambda bi, hi: (bi, 0, hi)),
        out_shape=jax.ShapeDtypeStruct((b, s, RET_HEADS * dk), BF16),
        scratch_shapes=[pltpu.VMEM((s // 2, dk), F32)] * 2 + [pltpu.VMEM((dk, dk), F32)] * 2
        + [pltpu.VMEM((c, dk), F32)] * 4 + [pltpu.VMEM((c, c), F32)],
        compiler_params=_params(("parallel", "parallel")),
        name="retention",
    )(decays, proj, proj, proj, proj)


def _out_proj_kernel(x_ref, mix_ref, q_ref, kv_ref, w1_ref, w2_ref, o_ref, *, q_w):
    scale = HEAD_DIM ** -0.5
    per = q_w // HEAD_DIM
    scores = []
    for hh in range(CROSS_HEADS):
        q = q_ref[0, hh // per][:, (hh % per) * HEAD_DIM:(hh % per + 1) * HEAD_DIM]
        scores.append(lax.dot_general(q, kv_ref[0, hh], (((1,), (1,)), ((), ())),
                                      preferred_element_type=F32) * scale)
    acc = jnp.dot(mix_ref[0], w1_ref[...], preferred_element_type=F32)
    outs = []
    for hh, s in enumerate(scores):
        m = jnp.max(s, axis=-1, keepdims=True)
        p = jnp.exp(s - m)
        l = jnp.sum(p, axis=-1, keepdims=True)
        o = jnp.dot(p.astype(BF16), kv_ref[0, CROSS_HEADS + hh], preferred_element_type=F32) / l
        outs.append(o.astype(BF16))
    cross = jnp.concatenate(outs, axis=1)
    acc = acc + jnp.dot(cross, w2_ref[...], preferred_element_type=F32)
    o_ref[0] = x_ref[0] + acc


def _out_proj(x, mix, proj, q_head0, kv, w1, w2, *, tm):
    b, s, d = x.shape
    km = mix.shape[-1]
    q_w = proj.shape[-1]
    nq = CROSS_DIM // q_w
    return pl.pallas_call(
        functools.partial(_out_proj_kernel, q_w=q_w),
        grid=(b, s // tm),
        in_specs=[
            pl.BlockSpec((1, tm, d), lambda bi, i: (bi, i, 0)),
            pl.BlockSpec((1, tm, km), lambda bi, i: (bi, i, 0)),
            pl.BlockSpec((1, nq, tm, q_w), lambda bi, i: (bi, q_head0 // nq, i, 0)),
            pl.BlockSpec((1, 2 * CROSS_HEADS, N_MEM, HEAD_DIM), lambda bi, i: (0, 0, bi, 0)),
            pl.BlockSpec((km, d), lambda bi, i: (0, 0)),
            pl.BlockSpec((CROSS_DIM, d), lambda bi, i: (0, 0)),
        ],
        out_specs=pl.BlockSpec((1, tm, d), lambda bi, i: (bi, i, 0)),
        out_shape=jax.ShapeDtypeStruct((b, s, d), F32),
        compiler_params=_params(("parallel", "parallel")),
        name="out_proj",
    )(x, mix, proj, kv, w1, w2)


def _ffn_kernel(x_ref, g_ref, gf_ref, wg_ref, wu_ref, wd_ref, o_ref, xn_ref, *, final_norm):
    f = pl.program_id(2)

    @pl.when(f == 0)
    def _():
        x = x_ref[0]
        xn_ref[...] = _rmsnorm_f32(x, g_ref[...]).astype(BF16)
        o_ref[0] = x

    xn = xn_ref[...]
    gate = jnp.dot(xn, wg_ref[...], preferred_element_type=F32)
    up = jnp.dot(xn, wu_ref[...], preferred_element_type=F32)
    act = (gate * jax.nn.sigmoid(gate)) * up
    o_ref[0] += jnp.dot(act.astype(BF16), wd_ref[...], preferred_element_type=F32)

    if final_norm:
        @pl.when(f == pl.num_programs(2) - 1)
        def _():
            o_ref[0] = _rmsnorm_f32(o_ref[0], gf_ref[...])


def _ffn(x, g, g_final, w_gate_up, w_down, *, final_norm, tm, tf=TN):
    b, s, d = x.shape
    nf = w_down.shape[0] // tf
    return pl.pallas_call(
        functools.partial(_ffn_kernel, final_norm=final_norm),
        grid=(b, s // tm, nf),
        in_specs=[
            pl.BlockSpec((1, tm, d), lambda bi, i, f: (bi, i, 0)),
            pl.BlockSpec((1, d), lambda bi, i, f: (0, 0)),
            pl.BlockSpec((1, d), lambda bi, i, f: (0, 0)),
            pl.BlockSpec((d, tf), lambda bi, i, f: (0, f)),
            pl.BlockSpec((d, tf), lambda bi, i, f: (0, nf + f)),
            pl.BlockSpec((tf, d), lambda bi, i, f: (f, 0)),
        ],
        out_specs=pl.BlockSpec((1, tm, d), lambda bi, i, f: (bi, i, 0)),
        out_shape=jax.ShapeDtypeStruct((b, s, d), F32),
        scratch_shapes=[pltpu.VMEM((tm, d), BF16)],
        compiler_params=_params(("parallel", "parallel", "arbitrary")),
        name="ffn",
    )(x, g.reshape(1, d), g_final.reshape(1, d), w_gate_up, w_gate_up, w_down)


def _trunk(x, mem, p):
    for i in range(2):
        kv = _norm_proj(mem.reshape(1, -1, mem.shape[-1]), p["norm_mem"][i], p["w_mem_kv"][i],
                        head_w=HEAD_DIM, tm=mem.shape[0] * N_MEM)
        if i == 0:
            projs = _norm_proj(x, p["norm_mix"][i], p["a_w_in"], head_w=HEAD_DIM, tm=TM_PROJ, tn=TN_A,
                               segments=p["a_segments"], q_steps=A_SLOTS * HEAD_DIM // TN_A, q_scale=ATT_Q_SCALE)
            mix = _dilated_attention(*projs)
            proj, q_head0 = projs[0], 3 * A_SLOTS
            w1, w2 = p["a_w_out1"], p["a_w_out2"]
        else:
            proj, q_head0 = _norm_proj(x, p["norm_mix"][i], p["b_w_in"], head_w=RET_DK, tm=TM_PROJ_B), 4 * RET_HEADS
            mix = _retention(proj, p["b_decay"])
            w1, w2 = p["b_w_out1"], p["b_w_out2"]
        x = _out_proj(x, mix, proj, q_head0, kv, w1, w2, tm=TM_OUT)
        x = _ffn(x, p["norm_ffn"][i], p["norm_final"], p["w_gate_up"][i], p["w_down"][i],
                 final_norm=(i == 1), tm=TM_FFN)
    return x


def _group_columns(w_in):
    gw = A_SLOTS * HEAD_DIM
    cols, segments = [], []
    for g, (_, dil) in enumerate(DIL_GROUPS):
        group = [w_in[:, (c * N_GROUPS + g) * gw:(c * N_GROUPS + g + 1) * gw] for c in range(3)]
        if g == 0:
            group.append(w_in[:, 3 * N_GROUPS * gw:])
        width = sum(w.shape[1] for w in group)
        pad = -width % TN_A
        if pad:
            group.append(jnp.zeros((w_in.shape[0], pad), w_in.dtype))
        cols += group
        segments.append(((width + pad) // TN_A, dil))
    return jnp.concatenate(cols, axis=1), tuple(segments)


def kernel(x_prompt, x_sample, mem_prompt, mem_sample, norm_mix, norm_mem, w_mem_kv, a_w_in, a_w_out,
           b_w_in, b_w_out, b_decay_fwd, b_decay_bwd, norm_ffn, w_gate_up, w_down, norm_final):
    a_mix = A_SLOTS * HEAD_DIM
    b_mix = RET_HEADS * RET_DK
    a_cols, a_segments = _group_columns(a_w_in[0])
    p = {
        "norm_mix": norm_mix, "norm_mem": norm_mem, "norm_ffn": norm_ffn, "norm_final": norm_final,
        "w_mem_kv": w_mem_kv.astype(BF16),
        "a_w_in": a_cols.astype(BF16), "a_segments": a_segments,
        "a_w_out1": a_w_out[0, :a_mix].astype(BF16), "a_w_out2": a_w_out[0, a_mix:].astype(BF16),
        "b_w_in": b_w_in[0].astype(BF16),
        "b_w_out1": b_w_out[0, :b_mix].astype(BF16), "b_w_out2": b_w_out[0, b_mix:].astype(BF16),
        "b_decay": jnp.stack([b_decay_fwd[0], b_decay_bwd[0]]).astype(F32),
        "w_gate_up": w_gate_up.astype(BF16), "w_down": w_down.astype(BF16),
    }
    return _trunk(x_prompt, mem_prompt, p), _trunk(x_sample, mem_sample, p)
```

```python
import functools

import jax
import jax.numpy as jnp
from jax import lax
from jax.experimental import pallas as pl
from jax.experimental.pallas import tpu as pltpu

F32 = jnp.float32
BF16 = jnp.bfloat16

D_MODEL = 2048
HEAD_DIM = 128
DIL_GROUPS = ((128, 1), (512, 4), (2048, 16))
N_GROUPS = len(DIL_GROUPS)
A_SLOTS = D_MODEL // 256
A_HEADS = N_GROUPS * A_SLOTS
RET_HEADS = D_MODEL // 256
RET_DK = 256
RET_CHUNK = 256
CROSS_HEADS = 4
CROSS_DIM = CROSS_HEADS * HEAD_DIM
N_MEM = 256
D_FF = 5632
EPS = 1e-6
NEG_BIG = -1e30
LOG2_E = 1.4426950408889634
ATT_Q_SCALE = HEAD_DIM ** -0.5 * LOG2_E

ATT_BLOCK = 128
ATT_HALF = 64
ATT_KEYS = 256
ATT_BATCH = 8
DIL_RATIO = 4

TM_PROJ = 1024
TM_PROJ_B = 2048
TM_OUT = 512
TM_FFN = 1024
TN = 512
TN_A = 1024

V7X_VMEM_LIMIT = 56 * 1024 * 1024
MXU_COLS = 256


def _params(semantics):
    return pltpu.CompilerParams(dimension_semantics=semantics, vmem_limit_bytes=V7X_VMEM_LIMIT)


def _rmsnorm_f32(x, g):
    ms = jnp.mean(x * x, axis=-1, keepdims=True)
    return (x * lax.rsqrt(ms + EPS)) * g


def _norm_proj_kernel(x_ref, g_ref, w_ref, *rest, segments, head_w, half_w, q_steps, q_scale):
    n_seg = len(segments)
    o_refs, xn_ref, scratch = rest[:n_seg], rest[n_seg], rest[n_seg + 1:]
    j = pl.program_id(2)
    tm, tn = xn_ref.shape[0], w_ref.shape[-1]
    per_half = half_w // head_w

    @pl.when(j == 0)
    def _():
        xn_ref[...] = _rmsnorm_f32(x_ref[0], g_ref[...]).astype(BF16)

    def project(half):
        return jnp.dot(xn_ref[...], w_ref[:, half * half_w:(half + 1) * half_w], preferred_element_type=F32)

    def deinterleave(o_ref, h, dil):
        stage_ref = scratch[0]
        if dil <= DIL_RATIO:
            for r in range(dil):
                o_ref[0, h, r] = stage_ref[h, pl.ds(r, tm // dil, stride=dil), :].astype(o_ref.dtype)
            return
        stage2_ref = scratch[1]
        outer = dil // DIL_RATIO
        for r1 in range(DIL_RATIO):
            stage2_ref[h, r1] = stage_ref[h, pl.ds(r1, tm // DIL_RATIO, stride=DIL_RATIO), :]
        for r1 in range(DIL_RATIO):
            for k in range(outer):
                o_ref[0, h, k * DIL_RATIO + r1] = (
                    stage2_ref[h, r1, pl.ds(k, tm // dil, stride=outer), :].astype(o_ref.dtype))

    start = 0
    for o_ref, (steps, dil) in zip(o_refs, segments):
        def segment(o_ref=o_ref, dil=dil, start=start):
            for half in range(tn // half_w):
                res = project(half)
                if q_steps:
                    res = res * jnp.where(j - start < q_steps, q_scale, 1.0)
                for hh in range(per_half):
                    h = half * per_half + hh
                    cols = res[:, hh * head_w:(hh + 1) * head_w]
                    if dil == 1:
                        o_ref[0, h] = cols.astype(o_ref.dtype)
                    else:
                        scratch[0][h] = cols
                if dil > 1:
                    for hh in range(per_half):
                        deinterleave(o_ref, half * per_half + hh, dil)

        if n_seg == 1:
            segment()
        else:
            pl.when((j >= start) & (j < start + steps))(segment)
        start += steps


def _norm_proj(x, g, w, *, head_w, tm, tn=TN, segments=None, q_steps=0, q_scale=1.0):
    b, s, d = x.shape
    nt = w.shape[1] // tn
    hps = tn // head_w
    segments = segments or ((nt, 1),)
    max_dil = max(dil for _, dil in segments)
    scratch = [pltpu.VMEM((tm, d), BF16)]
    if max_dil > 1:
        scratch.append(pltpu.VMEM((hps, tm, head_w), F32))
    if max_dil > DIL_RATIO:
        scratch.append(pltpu.VMEM((hps, DIL_RATIO, tm // DIL_RATIO, head_w), F32))
    out_specs, out_shapes, start = [], [], 0
    for steps, dil in segments:
        def held(j, start=start, steps=steps):
            return jnp.clip(j - start, 0, steps - 1)
        if dil == 1:
            out_specs.append(pl.BlockSpec((1, hps, tm, head_w), lambda bi, i, j, held=held: (bi, held(j), i, 0)))
            out_shapes.append(jax.ShapeDtypeStruct((b, steps * hps, s, head_w), BF16))
        else:
            out_specs.append(pl.BlockSpec((1, hps, dil, tm // dil, head_w),
                                          lambda bi, i, j, held=held: (bi, held(j), 0, i, 0)))
            out_shapes.append(jax.ShapeDtypeStruct((b, steps * hps, dil, s // dil, head_w), BF16))
        start += steps
    kern = functools.partial(_norm_proj_kernel, segments=tuple(segments), head_w=head_w,
                             half_w=max(head_w, MXU_COLS), q_steps=q_steps, q_scale=q_scale)
    outs = pl.pallas_call(
        kern,
        grid=(b, s // tm, nt),
        in_specs=[
            pl.BlockSpec((1, tm, d), lambda bi, i, j: (bi, i, 0)),
            pl.BlockSpec((1, d), lambda bi, i, j: (0, 0)),
            pl.BlockSpec((d, tn), lambda bi, i, j: (0, j)),
        ],
        out_specs=out_specs,
        out_shape=out_shapes,
        scratch_shapes=scratch,
        compiler_params=_params(("parallel", "parallel", "arbitrary")),
        name="norm_proj",
    )(x, g.reshape(1, d), w)
    return outs if len(outs) > 1 else outs[0]


def _dil_attn_kernel(q0, k0, v0, q1, k1, v1, q2, k2, v2, o_ref,
                     m1_ref, l1_ref, a1_ref, m0_ref, l0_ref, a0_ref, bias0_ref, bias1_ref, bias2_ref,
                     *, seq, unroll):
    h = pl.program_id(0)
    groups = ((q0, k0, v0, bias0_ref), (q1, k1, v1, bias1_ref), (q2, k2, v2, bias2_ref))
    stat = (ATT_BLOCK, HEAD_DIM)

    def build_bias(g):
        bias_ref = groups[g][3]
        dil = DIL_GROUPS[g][1]
        kw = bias_ref.shape[-1]
        jf = (h + (g * A_SLOTS + 1)).astype(F32)
        slope = jnp.exp2(jnp.full((ATT_BLOCK, kw), -8.0 / A_HEADS, F32) * jf)
        row = lax.broadcasted_iota(jnp.int32, (ATT_BLOCK, kw), 0)
        col = lax.broadcasted_iota(jnp.int32, (ATT_BLOCK, kw), 1)
        for t, delta in enumerate((0, -ATT_HALF, ATT_BLOCK - kw)):
            arel = jnp.abs(col - row + delta)
            bias_ref[t] = jnp.where(arel <= ATT_HALF, (-slope * (dil * arel).astype(F32)) * LOG2_E, NEG_BIG)

    def scores(g, q, k, blk, nblk):
        bias_ref = groups[g][3]
        table = jnp.where(blk == 0, 0, jnp.where(blk == nblk - 1, 2, 1))
        return lax.dot_general(q, k, (((1,), (1,)), ((), ())), preferred_element_type=F32) + bias_ref[table]

    def accumulate(s, v, state):
        m_new = jnp.broadcast_to(jnp.max(s, axis=-1, keepdims=True), stat)
        if state is not None:
            m_old, l_old, acc_old = state
            m_new = jnp.maximum(m_old, m_new)
            alpha = jnp.exp2(m_old - m_new)
        slabs = [s[:, c:c + HEAD_DIM] - m_new for c in range(0, s.shape[1], HEAD_DIM)]
        p = jnp.exp2(slabs[0] if len(slabs) == 1 else jnp.concatenate(slabs, axis=1))
        l_new = jnp.broadcast_to(jnp.sum(p, axis=-1, keepdims=True), stat)
        acc_new = jnp.dot(p.astype(BF16), v, preferred_element_type=F32)
        if state is not None:
            l_new = alpha * l_old + l_new
            acc_new = alpha * acc_old + acc_new
        return m_new, l_new, acc_new

    def run_group(g):
        q_ref, k_ref, v_ref, bias_ref = groups[g]
        dil = DIL_GROUPS[g][1]
        length = seq // dil
        kw = bias_ref.shape[-1]
        nblk = length // ATT_BLOCK
        shift = nblk.bit_length() - 1

        def score_block(it):
            r = it >> shift
            blk = it & (nblk - 1)
            i0 = pl.multiple_of(blk * ATT_BLOCK, ATT_BLOCK)
            ws = pl.multiple_of(jnp.clip(i0 - ATT_HALF, 0, length - kw), ATT_HALF)
            if g == 0:
                q = q_ref[0, 0, pl.ds(i0, ATT_BLOCK), :]
                k = k_ref[0, 0, pl.ds(ws, kw), :]
                v = v_ref[0, 0, pl.ds(ws, kw), :]
            else:
                q = q_ref[0, 0, r, pl.ds(i0, ATT_BLOCK), :]
                k = k_ref[0, 0, r, pl.ds(ws, kw), :]
                v = v_ref[0, 0, r, pl.ds(ws, kw), :]
            return r, i0, scores(g, q, k, blk, nblk), v

        def finish_block(r, i0, s, v):
            if g == 2:
                m, l, acc = accumulate(s, v, None)
                rows = pl.ds(i0 * DIL_RATIO + (r >> 2), ATT_BLOCK, stride=DIL_RATIO)
                m1_ref[r & 3, rows, :] = m
                l1_ref[r & 3, rows, :] = l
                a1_ref[r & 3, rows, :] = acc
            elif g == 1:
                rows = pl.ds(i0, ATT_BLOCK)
                m, l, acc = accumulate(s, v, (m1_ref[r, rows, :], l1_ref[r, rows, :], a1_ref[r, rows, :]))
                rows = pl.ds(i0 * DIL_RATIO + r, ATT_BLOCK, stride=DIL_RATIO)
                m0_ref[rows, :] = m
                l0_ref[rows, :] = l
                a0_ref[rows, :] = acc
            else:
                rows = pl.ds(i0, ATT_BLOCK)
                _, l, acc = accumulate(s, v, (m0_ref[rows, :], l0_ref[rows, :], a0_ref[rows, :]))
                o_ref[0, rows, :] = (acc / l).astype(o_ref.dtype)

        total = dil * nblk
        batch = min(ATT_BATCH, total) if kw < ATT_KEYS else 1

        def body(bi, carry):
            blocks = [score_block(bi * batch + i) for i in range(batch)]
            for blk_args in blocks:
                finish_block(*blk_args)
            return carry

        lax.fori_loop(0, total // batch, body, 0, unroll=max(1, min(unroll, total) // batch))

    @pl.when(pl.program_id(1) == 0)
    def _():
        for g in range(N_GROUPS):
            build_bias(g)

    run_group(2)
    run_group(1)
    run_group(0)


def _dilated_attention(p0, p1, p2, *, unroll=32):
    b, _, s, hd = p0.shape
    args, specs = [], []
    for c in range(3):
        args.append(p0)
        specs.append(pl.BlockSpec((1, 1, s, hd), lambda hi, bi, c=c: (bi, c * A_SLOTS + hi, 0, 0)))
    for pg in (p1, p2):
        dil, length = pg.shape[2], pg.shape[3]
        for c in range(3):
            args.append(pg)
            specs.append(pl.BlockSpec((1, 1, dil, length, hd), lambda hi, bi, c=c: (bi, c * A_SLOTS + hi, 0, 0, 0)))
    l1 = s // DIL_GROUPS[1][1]
    bias = [pltpu.VMEM((3, ATT_BLOCK, min(ATT_KEYS, s // dil)), F32) for _, dil in DIL_GROUPS]
    return pl.pallas_call(
        functools.partial(_dil_attn_kernel, seq=s, unroll=unroll),
        grid=(A_SLOTS, b),
        in_specs=specs,
        out_specs=pl.BlockSpec((1, s, hd), lambda hi, bi: (bi, 0, hi)),
        out_shape=jax.ShapeDtypeStruct((b, s, A_SLOTS * hd), BF16),
        scratch_shapes=[pltpu.VMEM((DIL_RATIO, l1, hd), F32)] * 3 + [pltpu.VMEM((s, hd), F32)] * 3 + bias,
        compiler_params=_params(("parallel", "arbitrary")),
        name="dilated_attention",
    )(*args)


def _retention_kernel(dec_ref, q_ref, k_ref, v_ref, gate_ref, o_ref,
                      yf_ref, yb_ref, sf_ref, sb_ref, qdf_ref, kdf_ref, qdb_ref, kdb_ref, intra_ref,
                      *, seq, unroll):
    h = pl.program_id(0)
    c = RET_CHUNK
    n = seq // c
    dk = RET_DK
    q_scale = dk ** -0.5

    def log_gamma(e, shape):
        return jnp.log1p(-jnp.exp2(-jnp.full(shape, e, F32)))

    e_f = dec_ref[0, h]
    e_b = dec_ref[1, h]
    cd_f = jnp.exp(c * log_gamma(e_f, (1, dk)))
    cd_b = jnp.exp(c * log_gamma(e_b, (1, dk)))

    @pl.when(pl.program_id(1) == 0)
    def _():
        lg_f = log_gamma(e_f, (c, dk))
        lg_b = log_gamma(e_b, (c, dk))
        i_row = lax.broadcasted_iota(jnp.int32, (c, dk), 0).astype(F32)
        qdf_ref[...] = jnp.exp((i_row + 1.0) * lg_f) * q_scale
        kdf_ref[...] = jnp.exp((c - 1.0 - i_row) * lg_f)
        qdb_ref[...] = jnp.exp((c - i_row) * lg_b) * q_scale
        kdb_ref[...] = jnp.exp(i_row * lg_b)
        t = lax.broadcasted_iota(jnp.int32, (c, c), 0)
        s = lax.broadcasted_iota(jnp.int32, (c, c), 1)
        diff = (t - s).astype(F32)
        intra_ref[...] = q_scale * jnp.where(
            t >= s, jnp.exp(jnp.where(t >= s, diff, 0.0) * log_gamma(e_f, (c, c))),
            jnp.exp(jnp.where(t < s, -diff, 0.0) * log_gamma(e_b, (c, c))))

    def load(ci):
        rows = pl.ds(pl.multiple_of(ci * c, c), c)
        q = q_ref[0, 0, rows, :].astype(F32)
        k = k_ref[0, 0, rows, :].astype(F32)
        v = v_ref[0, 0, rows, :]
        return rows, q, k, v

    def inter(q_dec, k_dec, v, st_ref, cd):
        y = jnp.dot(q_dec.astype(BF16), st_ref[...].astype(BF16), preferred_element_type=F32)
        kv = lax.dot_general(k_dec.astype(BF16), v, (((0,), (0,)), ((), ())), preferred_element_type=F32)
        st_ref[...] = cd * st_ref[...] + kv
        return y

    sf_ref[...] = jnp.zeros_like(sf_ref)
    sb_ref[...] = jnp.zeros_like(sb_ref)

    def finish(rows, y):
        y = y * lax.rsqrt(jnp.mean(y * y, axis=-1, keepdims=True) + EPS)
        gate = gate_ref[0, 0, rows, :].astype(F32)
        o_ref[0, rows, :] = (y * (gate * jax.nn.sigmoid(gate))).astype(o_ref.dtype)

    half = n // 2

    def step(ci, carry, *, second_half):
        cj = n - 1 - ci
        rows, q, k, v = load(ci)
        sc = lax.dot_general(q_ref[0, 0, rows, :], k_ref[0, 0, rows, :], (((1,), (1,)), ((), ())),
                             preferred_element_type=F32) * intra_ref[...]
        y_f = jnp.dot(sc.astype(BF16), v, preferred_element_type=F32)
        y_f = y_f + inter(q * qdf_ref[...], k * kdf_ref[...], v, sf_ref, cd_f)
        if second_half:
            finish(rows, y_f + yb_ref[pl.ds(pl.multiple_of((ci - half) * c, c), c), :])
        else:
            yf_ref[rows, :] = y_f
        rows, q, k, v = load(cj)
        y_b = inter(q * qdb_ref[...], k * kdb_ref[...], v, sb_ref, cd_b)
        if second_half:
            finish(rows, yf_ref[rows, :] + y_b)
        else:
            yb_ref[pl.ds(pl.multiple_of((cj - half) * c, c), c), :] = y_b
        return carry

    lax.fori_loop(0, half, functools.partial(step, second_half=False), 0, unroll=min(unroll, half))
    lax.fori_loop(half, n, functools.partial(step, second_half=True), 0, unroll=min(unroll, half))


def _retention(proj, decays, *, unroll=8):
    b, _, s, dk = proj.shape
    c = RET_CHUNK

    def head_spec(offset):
        return pl.BlockSpec((1, 1, s, dk), lambda hi, bi: (bi, offset + hi, 0, 0))

    return pl.pallas_call(
        functools.partial(_retention_kernel, seq=s, unroll=unroll),
        grid=(RET_HEADS, b),
        in_specs=[pl.BlockSpec(memory_space=pltpu.SMEM)] + [head_spec(RET_HEADS * i) for i in range(4)],
        out_specs=pl.BlockSpec((1, s, dk), lambda hi, bi: (bi, 0, hi)),
        out_shape=jax.ShapeDtypeStruct((b, s, RET_HEADS * dk), BF16),
        scratch_shapes=[pltpu.VMEM((s // 2, dk), F32)] * 2 + [pltpu.VMEM((dk, dk), F32)] * 2
        + [pltpu.VMEM((c, dk), F32)] * 4 + [pltpu.VMEM((c, c), F32)],
        compiler_params=_params(("parallel", "arbitrary")),
        name="retention",
    )(decays, proj, proj, proj, proj)


def _out_proj_kernel(x_ref, mix_ref, q_ref, kv_ref, w1_ref, w2_ref, o_ref, *, q_w):
    scale = HEAD_DIM ** -0.5
    per = q_w // HEAD_DIM
    scores = []
    for hh in range(CROSS_HEADS):
        q = q_ref[0, hh // per][:, (hh % per) * HEAD_DIM:(hh % per + 1) * HEAD_DIM]
        scores.append(lax.dot_general(q, kv_ref[0, hh], (((1,), (1,)), ((), ())),
                                      preferred_element_type=F32) * scale)
    acc = jnp.dot(mix_ref[0], w1_ref[...], preferred_element_type=F32)
    outs = []
    for hh, s in enumerate(scores):
        m = jnp.max(s, axis=-1, keepdims=True)
        p = jnp.exp(s - m)
        l = jnp.sum(p, axis=-1, keepdims=True)
        o = jnp.dot(p.astype(BF16), kv_ref[0, CROSS_HEADS + hh], preferred_element_type=F32) / l
        outs.append(o.astype(BF16))
    cross = jnp.concatenate(outs, axis=1)
    acc = acc + jnp.dot(cross, w2_ref[...], preferred_element_type=F32)
    o_ref[0] = x_ref[0] + acc


def _out_proj(x, mix, proj, q_head0, kv, w1, w2, *, tm):
    b, s, d = x.shape
    km = mix.shape[-1]
    q_w = proj.shape[-1]
    nq = CROSS_DIM // q_w
    return pl.pallas_call(
        functools.partial(_out_proj_kernel, q_w=q_w),
        grid=(b, s // tm),
        in_specs=[
            pl.BlockSpec((1, tm, d), lambda bi, i: (bi, i, 0)),
            pl.BlockSpec((1, tm, km), lambda bi, i: (bi, i, 0)),
            pl.BlockSpec((1, nq, tm, q_w), lambda bi, i: (bi, q_head0 // nq, i, 0)),
            pl.BlockSpec((1, 2 * CROSS_HEADS, N_MEM, HEAD_DIM), lambda bi, i: (0, 0, bi, 0)),
            pl.BlockSpec((km, d), lambda bi, i: (0, 0)),
            pl.BlockSpec((CROSS_DIM, d), lambda bi, i: (0, 0)),
        ],
        out_specs=pl.BlockSpec((1, tm, d), lambda bi, i: (bi, i, 0)),
        out_shape=jax.ShapeDtypeStruct((b, s, d), F32),
        compiler_params=_params(("parallel", "parallel")),
        name="out_proj",
    )(x, mix, proj, kv, w1, w2)


def _ffn_kernel(x_ref, g_ref, gf_ref, wg_ref, wu_ref, wd_ref, o_ref, xn_ref, *, final_norm):
    f = pl.program_id(2)

    @pl.when(f == 0)
    def _():
        x = x_ref[0]
        xn_ref[...] = _rmsnorm_f32(x, g_ref[...]).astype(BF16)
        o_ref[0] = x

    xn = xn_ref[...]
    gate = jnp.dot(xn, wg_ref[...], preferred_element_type=F32)
    up = jnp.dot(xn, wu_ref[...], preferred_element_type=F32)
    act = (gate * jax.nn.sigmoid(gate)) * up
    o_ref[0] += jnp.dot(act.astype(BF16), wd_ref[...], preferred_element_type=F32)

    if final_norm:
        @pl.when(f == pl.num_programs(2) - 1)
        def _():
            o_ref[0] = _rmsnorm_f32(o_ref[0], gf_ref[...])


def _ffn(x, g, g_final, w_gate_up, w_down, *, final_norm, tm, tf=TN):
    b, s, d = x.shape
    nf = w_down.shape[0] // tf
    return pl.pallas_call(
        functools.partial(_ffn_kernel, final_norm=final_norm),
        grid=(b, s // tm, nf),
        in_specs=[
            pl.BlockSpec((1, tm, d), lambda bi, i, f: (bi, i, 0)),
            pl.BlockSpec((1, d), lambda bi, i, f: (0, 0)),
            pl.BlockSpec((1, d), lambda bi, i, f: (0, 0)),
            pl.BlockSpec((d, tf), lambda bi, i, f: (0, f)),
            pl.BlockSpec((d, tf), lambda bi, i, f: (0, nf + f)),
            pl.BlockSpec((tf, d), lambda bi, i, f: (f, 0)),
        ],
        out_specs=pl.BlockSpec((1, tm, d), lambda bi, i, f: (bi, i, 0)),
        out_shape=jax.ShapeDtypeStruct((b, s, d), F32),
        scratch_shapes=[pltpu.VMEM((tm, d), BF16)],
        compiler_params=_params(("parallel", "parallel", "arbitrary")),
        name="ffn",
    )(x, g.reshape(1, d), g_final.reshape(1, d), w_gate_up, w_gate_up, w_down)


def _trunk(x, mem, p):
    for i in range(2):
        kv = _norm_proj(mem.reshape(1, -1, mem.shape[-1]), p["norm_mem"][i], p["w_mem_kv"][i],
                        head_w=HEAD_DIM, tm=mem.shape[0] * N_MEM)
        if i == 0:
            projs = _norm_proj(x, p["norm_mix"][i], p["a_w_in"], head_w=HEAD_DIM, tm=TM_PROJ, tn=TN_A,
                               segments=p["a_segments"], q_steps=A_SLOTS * HEAD_DIM // TN_A, q_scale=ATT_Q_SCALE)
            mix = _dilated_attention(*projs)
            proj, q_head0 = projs[0], 3 * A_SLOTS
            w1, w2 = p["a_w_out1"], p["a_w_out2"]
        else:
            proj, q_head0 = _norm_proj(x, p["norm_mix"][i], p["b_w_in"], head_w=RET_DK, tm=TM_PROJ_B), 4 * RET_HEADS
            mix = _retention(proj, p["b_decay"])
            w1, w2 = p["b_w_out1"], p["b_w_out2"]
        x = _out_proj(x, mix, proj, q_head0, kv, w1, w2, tm=TM_OUT)
        x = _ffn(x, p["norm_ffn"][i], p["norm_final"], p["w_gate_up"][i], p["w_down"][i],
                 final_norm=(i == 1), tm=TM_FFN)
    return x


def _group_columns(w_in):
    gw = A_SLOTS * HEAD_DIM
    cols, segments = [], []
    for g, (_, dil) in enumerate(DIL_GROUPS):
        group = [w_in[:, (c * N_GROUPS + g) * gw:(c * N_GROUPS + g + 1) * gw] for c in range(3)]
        if g == 0:
            group.append(w_in[:, 3 * N_GROUPS * gw:])
        width = sum(w.shape[1] for w in group)
        pad = -width % TN_A
        if pad:
            group.append(jnp.zeros((w_in.shape[0], pad), w_in.dtype))
        cols += group
        segments.append(((width + pad) // TN_A, dil))
    return jnp.concatenate(cols, axis=1), tuple(segments)


def kernel(x_prompt, x_sample, mem_prompt, mem_sample, norm_mix, norm_mem, w_mem_kv, a_w_in, a_w_out,
           b_w_in, b_w_out, b_decay_fwd, b_decay_bwd, norm_ffn, w_gate_up, w_down, norm_final):
    a_mix = A_SLOTS * HEAD_DIM
    b_mix = RET_HEADS * RET_DK
    a_cols, a_segments = _group_columns(a_w_in[0])
    p = {
        "norm_mix": norm_mix, "norm_mem": norm_mem, "norm_ffn": norm_ffn, "norm_final": norm_final,
        "w_mem_kv": w_mem_kv.astype(BF16),
        "a_w_in": a_cols.astype(BF16), "a_segments": a_segments,
        "a_w_out1": a_w_out[0, :a_mix].astype(BF16), "a_w_out2": a_w_out[0, a_mix:].astype(BF16),
        "b_w_in": b_w_in[0].astype(BF16),
        "b_w_out1": b_w_out[0, :b_mix].astype(BF16), "b_w_out2": b_w_out[0, b_mix:].astype(BF16),
        "b_decay": jnp.stack([b_decay_fwd[0], b_decay_bwd[0]]).astype(F32),
        "w_gate_up": w_gate_up.astype(BF16), "w_down": w_down.astype(BF16),
    }
    return _trunk(x_prompt, mem_prompt, p), _trunk(x_sample, mem_sample, p)
```

```python
import functools

import jax
import jax.numpy as jnp
from jax import lax
from jax.experimental import pallas as pl
from jax.experimental.pallas import tpu as pltpu

F32 = jnp.float32
BF16 = jnp.bfloat16

D_MODEL = 2048
HEAD_DIM = 128
DIL_GROUPS = ((128, 1), (512, 4), (2048, 16))
N_GROUPS = len(DIL_GROUPS)
A_SLOTS = D_MODEL // 256
A_HEADS = N_GROUPS * A_SLOTS
RET_HEADS = D_MODEL // 256
RET_DK = 256
RET_CHUNK = 256
CROSS_HEADS = 4
CROSS_DIM = CROSS_HEADS * HEAD_DIM
N_MEM = 256
D_FF = 5632
EPS = 1e-6
NEG_BIG = -1e30
LOG2_E = 1.4426950408889634
ATT_Q_SCALE = HEAD_DIM ** -0.5 * LOG2_E

ATT_BLOCK = 128
ATT_HALF = 64
ATT_KEYS = 256
ATT_BATCH = 8
DIL_RATIO = 4

TM_PROJ = 1024
TM_PROJ_B = 2048
TM_OUT = 512
TM_FFN = 1024
TN = 512
TN_A = 1024

V7X_VMEM_LIMIT = 56 * 1024 * 1024
MXU_COLS = 256


def _params(semantics):
    return pltpu.CompilerParams(dimension_semantics=semantics, vmem_limit_bytes=V7X_VMEM_LIMIT)


def _rmsnorm_f32(x, g):
    ms = jnp.mean(x * x, axis=-1, keepdims=True)
    return (x * lax.rsqrt(ms + EPS)) * g


def _norm_proj_kernel(x_ref, g_ref, w_ref, *rest, segments, head_w, half_w, q_steps, q_scale):
    n_seg = len(segments)
    o_refs, xn_ref, scratch = rest[:n_seg], rest[n_seg], rest[n_seg + 1:]
    j = pl.program_id(2)
    tm, tn = xn_ref.shape[0], w_ref.shape[-1]
    per_half = half_w // head_w

    @pl.when(j == 0)
    def _():
        xn_ref[...] = _rmsnorm_f32(x_ref[0], g_ref[...]).astype(BF16)

    def project(half):
        return jnp.dot(xn_ref[...], w_ref[:, half * half_w:(half + 1) * half_w], preferred_element_type=F32)

    def deinterleave(o_ref, h, dil):
        stage_ref = scratch[0]
        if dil <= DIL_RATIO:
            for r in range(dil):
                o_ref[0, h, r] = stage_ref[h, pl.ds(r, tm // dil, stride=dil), :].astype(o_ref.dtype)
            return
        stage2_ref = scratch[1]
        outer = dil // DIL_RATIO
        for r1 in range(DIL_RATIO):
            stage2_ref[h, r1] = stage_ref[h, pl.ds(r1, tm // DIL_RATIO, stride=DIL_RATIO), :]
        for r1 in range(DIL_RATIO):
            for k in range(outer):
                o_ref[0, h, k * DIL_RATIO + r1] = (
                    stage2_ref[h, r1, pl.ds(k, tm // dil, stride=outer), :].astype(o_ref.dtype))

    start = 0
    for o_ref, (steps, dil) in zip(o_refs, segments):
        def segment(o_ref=o_ref, dil=dil, start=start):
            for half in range(tn // half_w):
                res = project(half)
                if q_steps:
                    res = res * jnp.where(j - start < q_steps, q_scale, 1.0)
                for hh in range(per_half):
                    h = half * per_half + hh
                    cols = res[:, hh * head_w:(hh + 1) * head_w]
                    if dil == 1:
                        o_ref[0, h] = cols.astype(o_ref.dtype)
                    else:
                        scratch[0][h] = cols
                if dil > 1:
                    for hh in range(per_half):
                        deinterleave(o_ref, half * per_half + hh, dil)

        if n_seg == 1:
            segment()
        else:
            pl.when((j >= start) & (j < start + steps))(segment)
        start += steps


def _norm_proj(x, g, w, *, head_w, tm, tn=TN, segments=None, q_steps=0, q_scale=1.0):
    b, s, d = x.shape
    nt = w.shape[1] // tn
    hps = tn // head_w
    segments = segments or ((nt, 1),)
    max_dil = max(dil for _, dil in segments)
    assert s % tm == 0 and w.shape == (d, nt * tn) and tn % max(head_w, MXU_COLS) == 0
    assert sum(steps for steps, _ in segments) == nt
    assert tm % (max_dil * 16) == 0
    scratch = [pltpu.VMEM((tm, d), BF16)]
    if max_dil > 1:
        scratch.append(pltpu.VMEM((hps, tm, head_w), F32))
    if max_dil > DIL_RATIO:
        scratch.append(pltpu.VMEM((hps, DIL_RATIO, tm // DIL_RATIO, head_w), F32))
    out_specs, out_shapes, start = [], [], 0
    for steps, dil in segments:
        def held(j, start=start, steps=steps):
            return jnp.clip(j - start, 0, steps - 1)
        if dil == 1:
            out_specs.append(pl.BlockSpec((1, hps, tm, head_w), lambda bi, i, j, held=held: (bi, held(j), i, 0)))
            out_shapes.append(jax.ShapeDtypeStruct((b, steps * hps, s, head_w), BF16))
        else:
            out_specs.append(pl.BlockSpec((1, hps, dil, tm // dil, head_w),
                                          lambda bi, i, j, held=held: (bi, held(j), 0, i, 0)))
            out_shapes.append(jax.ShapeDtypeStruct((b, steps * hps, dil, s // dil, head_w), BF16))
        start += steps
    kern = functools.partial(_norm_proj_kernel, segments=tuple(segments), head_w=head_w,
                             half_w=max(head_w, MXU_COLS), q_steps=q_steps, q_scale=q_scale)
    outs = pl.pallas_call(
        kern,
        grid=(b, s // tm, nt),
        in_specs=[
            pl.BlockSpec((1, tm, d), lambda bi, i, j: (bi, i, 0)),
            pl.BlockSpec((1, d), lambda bi, i, j: (0, 0)),
            pl.BlockSpec((d, tn), lambda bi, i, j: (0, j)),
        ],
        out_specs=out_specs,
        out_shape=out_shapes,
        scratch_shapes=scratch,
        compiler_params=_params(("parallel", "parallel", "arbitrary")),
        name="norm_proj",
    )(x, g.reshape(1, d), w)
    return outs if len(outs) > 1 else outs[0]


def _dil_attn_kernel(q0, k0, v0, q1, k1, v1, q2, k2, v2, o_ref,
                     m1_ref, l1_ref, a1_ref, m0_ref, l0_ref, a0_ref, bias0_ref, bias1_ref, bias2_ref,
                     *, seq, unroll):
    h = pl.program_id(0)
    groups = ((q0, k0, v0, bias0_ref), (q1, k1, v1, bias1_ref), (q2, k2, v2, bias2_ref))
    stat = (ATT_BLOCK, HEAD_DIM)

    def build_bias(g):
        bias_ref = groups[g][3]
        dil = DIL_GROUPS[g][1]
        kw = bias_ref.shape[-1]
        jf = (h + (g * A_SLOTS + 1)).astype(F32)
        slope = jnp.exp2(jnp.full((ATT_BLOCK, kw), -8.0 / A_HEADS, F32) * jf)
        row = lax.broadcasted_iota(jnp.int32, (ATT_BLOCK, kw), 0)
        col = lax.broadcasted_iota(jnp.int32, (ATT_BLOCK, kw), 1)
        for t, delta in enumerate((0, -ATT_HALF, ATT_BLOCK - kw)):
            arel = jnp.abs(col - row + delta)
            bias_ref[t] = jnp.where(arel <= ATT_HALF, (-slope * (dil * arel).astype(F32)) * LOG2_E, NEG_BIG)

    def scores(g, q, k, blk, nblk):
        bias_ref = groups[g][3]
        table = jnp.where(blk == 0, 0, jnp.where(blk == nblk - 1, 2, 1))
        return lax.dot_general(q, k, (((1,), (1,)), ((), ())), preferred_element_type=F32) + bias_ref[table]

    def accumulate(s, v, state):
        m_new = jnp.broadcast_to(jnp.max(s, axis=-1, keepdims=True), stat)
        if state is not None:
            m_old, l_old, acc_old = state
            m_new = jnp.maximum(m_old, m_new)
            alpha = jnp.exp2(m_old - m_new)
        slabs = [s[:, c:c + HEAD_DIM] - m_new for c in range(0, s.shape[1], HEAD_DIM)]
        p = jnp.exp2(slabs[0] if len(slabs) == 1 else jnp.concatenate(slabs, axis=1))
        l_new = jnp.broadcast_to(jnp.sum(p, axis=-1, keepdims=True), stat)
        acc_new = jnp.dot(p.astype(BF16), v, preferred_element_type=F32)
        if state is not None:
            l_new = alpha * l_old + l_new
            acc_new = alpha * acc_old + acc_new
        return m_new, l_new, acc_new

    def run_group(g):
        q_ref, k_ref, v_ref, bias_ref = groups[g]
        dil = DIL_GROUPS[g][1]
        length = seq // dil
        kw = bias_ref.shape[-1]
        nblk = length // ATT_BLOCK
        shift = nblk.bit_length() - 1

        def score_block(it):
            r = it >> shift
            blk = it & (nblk - 1)
            i0 = pl.multiple_of(blk * ATT_BLOCK, ATT_BLOCK)
            ws = pl.multiple_of(jnp.clip(i0 - ATT_HALF, 0, length - kw), ATT_HALF)
            if g == 0:
                q = q_ref[0, 0, pl.ds(i0, ATT_BLOCK), :]
                k = k_ref[0, 0, pl.ds(ws, kw), :]
                v = v_ref[0, 0, pl.ds(ws, kw), :]
            else:
                q = q_ref[0, 0, r, pl.ds(i0, ATT_BLOCK), :]
                k = k_ref[0, 0, r, pl.ds(ws, kw), :]
                v = v_ref[0, 0, r, pl.ds(ws, kw), :]
            return r, i0, scores(g, q, k, blk, nblk), v

        def finish_block(r, i0, s, v):
            if g == 2:
                m, l, acc = accumulate(s, v, None)
                rows = pl.ds(i0 * DIL_RATIO + (r >> 2), ATT_BLOCK, stride=DIL_RATIO)
                m1_ref[r & 3, rows, :] = m
                l1_ref[r & 3, rows, :] = l
                a1_ref[r & 3, rows, :] = acc
            elif g == 1:
                rows = pl.ds(i0, ATT_BLOCK)
                m, l, acc = accumulate(s, v, (m1_ref[r, rows, :], l1_ref[r, rows, :], a1_ref[r, rows, :]))
                rows = pl.ds(i0 * DIL_RATIO + r, ATT_BLOCK, stride=DIL_RATIO)
                m0_ref[rows, :] = m
                l0_ref[rows, :] = l
                a0_ref[rows, :] = acc
            else:
                rows = pl.ds(i0, ATT_BLOCK)
                _, l, acc = accumulate(s, v, (m0_ref[rows, :], l0_ref[rows, :], a0_ref[rows, :]))
                o_ref[0, rows, :] = (acc / l).astype(o_ref.dtype)

        total = dil * nblk
        batch = min(ATT_BATCH, total) if kw < ATT_KEYS else 1

        def body(bi, carry):
            blocks = [score_block(bi * batch + i) for i in range(batch)]
            for blk_args in blocks:
                finish_block(*blk_args)
            return carry

        lax.fori_loop(0, total // batch, body, 0, unroll=max(1, min(unroll, total) // batch))

    @pl.when(pl.program_id(1) == 0)
    def _():
        for g in range(N_GROUPS):
            build_bias(g)

    run_group(2)
    run_group(1)
    run_group(0)


def _dilated_attention(p0, p1, p2, *, unroll=32):
    b, _, s, hd = p0.shape
    n_far = s // (DIL_GROUPS[-1][1] * ATT_BLOCK)
    assert n_far >= 1 and s % (DIL_GROUPS[-1][1] * ATT_BLOCK) == 0 and n_far & (n_far - 1) == 0
    assert all(window // (2 * dil) == ATT_HALF for window, dil in DIL_GROUPS)
    assert all(DIL_GROUPS[g + 1][1] == DIL_RATIO * DIL_GROUPS[g][1] for g in range(N_GROUPS - 1))
    args, specs = [], []
    for c in range(3):
        args.append(p0)
        specs.append(pl.BlockSpec((1, 1, s, hd), lambda hi, bi, c=c: (bi, c * A_SLOTS + hi, 0, 0)))
    for pg in (p1, p2):
        dil, length = pg.shape[2], pg.shape[3]
        for c in range(3):
            args.append(pg)
            specs.append(pl.BlockSpec((1, 1, dil, length, hd), lambda hi, bi, c=c: (bi, c * A_SLOTS + hi, 0, 0, 0)))
    l1 = s // DIL_GROUPS[1][1]
    bias = [pltpu.VMEM((3, ATT_BLOCK, min(ATT_KEYS, s // dil)), F32) for _, dil in DIL_GROUPS]
    return pl.pallas_call(
        functools.partial(_dil_attn_kernel, seq=s, unroll=unroll),
        grid=(A_SLOTS, b),
        in_specs=specs,
        out_specs=pl.BlockSpec((1, s, hd), lambda hi, bi: (bi, 0, hi)),
        out_shape=jax.ShapeDtypeStruct((b, s, A_SLOTS * hd), BF16),
        scratch_shapes=[pltpu.VMEM((DIL_RATIO, l1, hd), F32)] * 3 + [pltpu.VMEM((s, hd), F32)] * 3 + bias,
        compiler_params=_params(("parallel", "arbitrary")),
        name="dilated_attention",
    )(*args)


def _retention_kernel(dec_ref, q_ref, k_ref, v_ref, gate_ref, o_ref,
                      yf_ref, yb_ref, sf_ref, sb_ref, qdf_ref, kdf_ref, qdb_ref, kdb_ref, intra_ref,
                      *, seq, unroll):
    h = pl.program_id(0)
    c = RET_CHUNK
    n = seq // c
    dk = RET_DK
    q_scale = dk ** -0.5

    def log_gamma(e, shape):
        return jnp.log1p(-jnp.exp2(-jnp.full(shape, e, F32)))

    e_f = dec_ref[0, h]
    e_b = dec_ref[1, h]
    cd_f = jnp.exp(c * log_gamma(e_f, (1, dk)))
    cd_b = jnp.exp(c * log_gamma(e_b, (1, dk)))

    @pl.when(pl.program_id(1) == 0)
    def _():
        lg_f = log_gamma(e_f, (c, dk))
        lg_b = log_gamma(e_b, (c, dk))
        i_row = lax.broadcasted_iota(jnp.int32, (c, dk), 0).astype(F32)
        qdf_ref[...] = jnp.exp((i_row + 1.0) * lg_f) * q_scale
        kdf_ref[...] = jnp.exp((c - 1.0 - i_row) * lg_f)
        qdb_ref[...] = jnp.exp((c - i_row) * lg_b) * q_scale
        kdb_ref[...] = jnp.exp(i_row * lg_b)
        t = lax.broadcasted_iota(jnp.int32, (c, c), 0)
        s = lax.broadcasted_iota(jnp.int32, (c, c), 1)
        diff = (t - s).astype(F32)
        intra_ref[...] = q_scale * jnp.where(
            t >= s, jnp.exp(jnp.where(t >= s, diff, 0.0) * log_gamma(e_f, (c, c))),
            jnp.exp(jnp.where(t < s, -diff, 0.0) * log_gamma(e_b, (c, c))))

    def load(ci):
        rows = pl.ds(pl.multiple_of(ci * c, c), c)
        q = q_ref[0, 0, rows, :].astype(F32)
        k = k_ref[0, 0, rows, :].astype(F32)
        v = v_ref[0, 0, rows, :]
        return rows, q, k, v

    def inter(q_dec, k_dec, v, st_ref, cd):
        y = jnp.dot(q_dec.astype(BF16), st_ref[...].astype(BF16), preferred_element_type=F32)
        kv = lax.dot_general(k_dec.astype(BF16), v, (((0,), (0,)), ((), ())), preferred_element_type=F32)
        st_ref[...] = cd * st_ref[...] + kv
        return y

    sf_ref[...] = jnp.zeros_like(sf_ref)
    sb_ref[...] = jnp.zeros_like(sb_ref)

    def finish(rows, y):
        y = y * lax.rsqrt(jnp.mean(y * y, axis=-1, keepdims=True) + EPS)
        gate = gate_ref[0, 0, rows, :].astype(F32)
        o_ref[0, rows, :] = (y * (gate * jax.nn.sigmoid(gate))).astype(o_ref.dtype)

    half = n // 2

    def step(ci, carry, *, second_half):
        cj = n - 1 - ci
        rows, q, k, v = load(ci)
        sc = lax.dot_general(q_ref[0, 0, rows, :], k_ref[0, 0, rows, :], (((1,), (1,)), ((), ())),
                             preferred_element_type=F32) * intra_ref[...]
        y_f = jnp.dot(sc.astype(BF16), v, preferred_element_type=F32)
        y_f = y_f + inter(q * qdf_ref[...], k * kdf_ref[...], v, sf_ref, cd_f)
        if second_half:
            finish(rows, y_f + yb_ref[pl.ds(pl.multiple_of((ci - half) * c, c), c), :])
        else:
            yf_ref[rows, :] = y_f
        rows, q, k, v = load(cj)
        y_b = inter(q * qdb_ref[...], k * kdb_ref[...], v, sb_ref, cd_b)
        if second_half:
            finish(rows, yf_ref[rows, :] + y_b)
        else:
            yb_ref[pl.ds(pl.multiple_of((cj - half) * c, c), c), :] = y_b
        return carry

    lax.fori_loop(0, half, functools.partial(step, second_half=False), 0, unroll=min(unroll, half))
    lax.fori_loop(half, n, functools.partial(step, second_half=True), 0, unroll=min(unroll, half))


def _retention(proj, decays, *, unroll=8):
    b, _, s, dk = proj.shape
    c = RET_CHUNK
    assert dk == RET_DK and s % (2 * c) == 0

    def head_spec(offset):
        return pl.BlockSpec((1, 1, s, dk), lambda hi, bi: (bi, offset + hi, 0, 0))

    return pl.pallas_call(
        functools.partial(_retention_kernel, seq=s, unroll=unroll),
        grid=(RET_HEADS, b),
        in_specs=[pl.BlockSpec(memory_space=pltpu.SMEM)] + [head_spec(RET_HEADS * i) for i in range(4)],
        out_specs=pl.BlockSpec((1, s, dk), lambda hi, bi: (bi, 0, hi)),
        out_shape=jax.ShapeDtypeStruct((b, s, RET_HEADS * dk), BF16),
        scratch_shapes=[pltpu.VMEM((s // 2, dk), F32)] * 2 + [pltpu.VMEM((dk, dk), F32)] * 2
        + [pltpu.VMEM((c, dk), F32)] * 4 + [pltpu.VMEM((c, c), F32)],
        compiler_params=_params(("parallel", "arbitrary")),
        name="retention",
    )(decays, proj, proj, proj, proj)


def _out_proj_kernel(x_ref, mix_ref, q_ref, kv_ref, w1_ref, w2_ref, o_ref, *, q_w):
    scale = HEAD_DIM ** -0.5
    per = q_w // HEAD_DIM
    scores = []
    for hh in range(CROSS_HEADS):
        q = q_ref[0, hh // per][:, (hh % per) * HEAD_DIM:(hh % per + 1) * HEAD_DIM]
        scores.append(lax.dot_general(q, kv_ref[0, hh], (((1,), (1,)), ((), ())),
                                      preferred_element_type=F32) * scale)
    acc = jnp.dot(mix_ref[0], w1_ref[...], preferred_element_type=F32)
    outs = []
    for hh, s in enumerate(scores):
        m = jnp.max(s, axis=-1, keepdims=True)
        p = jnp.exp(s - m)
        l = jnp.sum(p, axis=-1, keepdims=True)
        o = jnp.dot(p.astype(BF16), kv_ref[0, CROSS_HEADS + hh], preferred_element_type=F32) / l
        outs.append(o.astype(BF16))
    cross = jnp.concatenate(outs, axis=1)
    acc = acc + jnp.dot(cross, w2_ref[...], preferred_element_type=F32)
    o_ref[0] = x_ref[0] + acc


def _out_proj(x, mix, proj, q_head0, kv, w1, w2, *, tm):
    b, s, d = x.shape
    km = mix.shape[-1]
    q_w = proj.shape[-1]
    nq = CROSS_DIM // q_w
    assert s % tm == 0 and q_head0 % nq == 0 and w1.shape == (km, d) and w2.shape == (CROSS_DIM, d)
    return pl.pallas_call(
        functools.partial(_out_proj_kernel, q_w=q_w),
        grid=(b, s // tm),
        in_specs=[
            pl.BlockSpec((1, tm, d), lambda bi, i: (bi, i, 0)),
            pl.BlockSpec((1, tm, km), lambda bi, i: (bi, i, 0)),
            pl.BlockSpec((1, nq, tm, q_w), lambda bi, i: (bi, q_head0 // nq, i, 0)),
            pl.BlockSpec((1, 2 * CROSS_HEADS, N_MEM, HEAD_DIM), lambda bi, i: (0, 0, bi, 0)),
            pl.BlockSpec((km, d), lambda bi, i: (0, 0)),
            pl.BlockSpec((CROSS_DIM, d), lambda bi, i: (0, 0)),
        ],
        out_specs=pl.BlockSpec((1, tm, d), lambda bi, i: (bi, i, 0)),
        out_shape=jax.ShapeDtypeStruct((b, s, d), F32),
        compiler_params=_params(("parallel", "parallel")),
        name="out_proj",
    )(x, mix, proj, kv, w1, w2)


def _ffn_kernel(x_ref, g_ref, gf_ref, wg_ref, wu_ref, wd_ref, o_ref, xn_ref, *, final_norm):
    f = pl.program_id(2)

    @pl.when(f == 0)
    def _():
        x = x_ref[0]
        xn_ref[...] = _rmsnorm_f32(x, g_ref[...]).astype(BF16)
        o_ref[0] = x

    xn = xn_ref[...]
    gate = jnp.dot(xn, wg_ref[...], preferred_element_type=F32)
    up = jnp.dot(xn, wu_ref[...], preferred_element_type=F32)
    act = (gate * jax.nn.sigmoid(gate)) * up
    o_ref[0] += jnp.dot(act.astype(BF16), wd_ref[...], preferred_element_type=F32)

    if final_norm:
        @pl.when(f == pl.num_programs(2) - 1)
        def _():
            o_ref[0] = _rmsnorm_f32(o_ref[0], gf_ref[...])


def _ffn(x, g, g_final, w_gate_up, w_down, *, final_norm, tm, tf=TN):
    b, s, d = x.shape
    nf = w_down.shape[0] // tf
    assert s % tm == 0 and w_down.shape == (nf * tf, d) and w_gate_up.shape == (d, 2 * nf * tf)
    return pl.pallas_call(
        functools.partial(_ffn_kernel, final_norm=final_norm),
        grid=(b, s // tm, nf),
        in_specs=[
            pl.BlockSpec((1, tm, d), lambda bi, i, f: (bi, i, 0)),
            pl.BlockSpec((1, d), lambda bi, i, f: (0, 0)),
            pl.BlockSpec((1, d), lambda bi, i, f: (0, 0)),
            pl.BlockSpec((d, tf), lambda bi, i, f: (0, f)),
            pl.BlockSpec((d, tf), lambda bi, i, f: (0, nf + f)),
            pl.BlockSpec((tf, d), lambda bi, i, f: (f, 0)),
        ],
        out_specs=pl.BlockSpec((1, tm, d), lambda bi, i, f: (bi, i, 0)),
        out_shape=jax.ShapeDtypeStruct((b, s, d), F32),
        scratch_shapes=[pltpu.VMEM((tm, d), BF16)],
        compiler_params=_params(("parallel", "parallel", "arbitrary")),
        name="ffn",
    )(x, g.reshape(1, d), g_final.reshape(1, d), w_gate_up, w_gate_up, w_down)


def _trunk(x, mem, p):
    for i in range(2):
        kv = _norm_proj(mem.reshape(1, -1, mem.shape[-1]), p["norm_mem"][i], p["w_mem_kv"][i],
                        head_w=HEAD_DIM, tm=mem.shape[0] * N_MEM)
        if i == 0:
            projs = _norm_proj(x, p["norm_mix"][i], p["a_w_in"], head_w=HEAD_DIM, tm=TM_PROJ, tn=TN_A,
                               segments=p["a_segments"], q_steps=A_SLOTS * HEAD_DIM // TN_A, q_scale=ATT_Q_SCALE)
            mix = _dilated_attention(*projs)
            proj, q_head0 = projs[0], 3 * A_SLOTS
            w1, w2 = p["a_w_out1"], p["a_w_out2"]
        else:
            proj, q_head0 = _norm_proj(x, p["norm_mix"][i], p["b_w_in"], head_w=RET_DK, tm=TM_PROJ_B), 4 * RET_HEADS
            mix = _retention(proj, p["b_decay"])
            w1, w2 = p["b_w_out1"], p["b_w_out2"]
        x = _out_proj(x, mix, proj, q_head0, kv, w1, w2, tm=TM_OUT)
        x = _ffn(x, p["norm_ffn"][i], p["norm_final"], p["w_gate_up"][i], p["w_down"][i],
                 final_norm=(i == 1), tm=TM_FFN)
    return x


def _group_columns(w_in):
    gw = A_SLOTS * HEAD_DIM
    cols, segments = [], []
    for g, (_, dil) in enumerate(DIL_GROUPS):
        group = [w_in[:, (c * N_GROUPS + g) * gw:(c * N_GROUPS + g + 1) * gw] for c in range(3)]
        if g == 0:
            group.append(w_in[:, 3 * N_GROUPS * gw:])
        width = sum(w.shape[1] for w in group)
        pad = -width % TN_A
        if pad:
            group.append(jnp.zeros((w_in.shape[0], pad), w_in.dtype))
        cols += group
        segments.append(((width + pad) // TN_A, dil))
    return jnp.concatenate(cols, axis=1), tuple(segments)


def kernel(x_prompt, x_sample, mem_prompt, mem_sample, norm_mix, norm_mem, w_mem_kv, a_w_in, a_w_out,
           b_w_in, b_w_out, b_decay_fwd, b_decay_bwd, norm_ffn, w_gate_up, w_down, norm_final):
    a_mix = A_SLOTS * HEAD_DIM
    b_mix = RET_HEADS * RET_DK
    assert x_prompt.shape[-1] == D_MODEL and x_sample.shape[-1] == D_MODEL
    assert mem_prompt.shape[1:] == (N_MEM, D_MODEL) and mem_sample.shape[1:] == (N_MEM, D_MODEL)
    assert norm_mix.shape[0] == 2 and a_w_in.shape[0] == 1 and b_w_in.shape[0] == 1
    a_cols, a_segments = _group_columns(a_w_in[0])
    p = {
        "norm_mix": norm_mix, "norm_mem": norm_mem, "norm_ffn": norm_ffn, "norm_final": norm_final,
        "w_mem_kv": w_mem_kv.astype(BF16),
        "a_w_in": a_cols.astype(BF16), "a_segments": a_segments,
        "a_w_out1": a_w_out[0, :a_mix].astype(BF16), "a_w_out2": a_w_out[0, a_mix:].astype(BF16),
        "b_w_in": b_w_in[0].astype(BF16),
        "b_w_out1": b_w_out[0, :b_mix].astype(BF16), "b_w_out2": b_w_out[0, b_mix:].astype(BF16),
        "b_decay": jnp.stack([b_decay_fwd[0], b_decay_bwd[0]]).astype(F32),
        "w_gate_up": w_gate_up.astype(BF16), "w_down": w_down.astype(BF16),
    }
    return _trunk(x_prompt, mem_prompt, p), _trunk(x_sample, mem_sample, p)
```

```python
import functools

import jax
import jax.numpy as jnp
from jax import lax
from jax.experimental import pallas as pl
from jax.experimental.pallas import tpu as pltpu

F32 = jnp.float32
BF16 = jnp.bfloat16

D_MODEL = 2048
HEAD_DIM = 128
DIL_GROUPS = ((128, 1), (512, 4), (2048, 16))
N_GROUPS = len(DIL_GROUPS)
A_SLOTS = D_MODEL // 256
A_HEADS = N_GROUPS * A_SLOTS
RET_HEADS = D_MODEL // 256
RET_DK = 256
RET_CHUNK = 256
CROSS_HEADS = 4
CROSS_DIM = CROSS_HEADS * HEAD_DIM
N_MEM = 256
D_FF = 5632
EPS = 1e-6
NEG_BIG = -1e30
LOG2_E = 1.4426950408889634
ATT_Q_SCALE = HEAD_DIM ** -0.5 * LOG2_E

ATT_BLOCK = 128
ATT_HALF = 64
ATT_KEYS = 256
ATT_BATCH = 8
DIL_RATIO = 4

TM_PROJ = 1024
TM_PROJ_B = 2048
TM_OUT = 512
TM_FFN = 1024
TN = 512
TN_A = 1024

V7X_VMEM_LIMIT = 56 * 1024 * 1024
MXU_COLS = 256


def _params(semantics):
    return pltpu.CompilerParams(dimension_semantics=semantics, vmem_limit_bytes=V7X_VMEM_LIMIT)


def _rmsnorm_f32(x, g):
    ms = jnp.mean(x * x, axis=-1, keepdims=True)
    return (x * lax.rsqrt(ms + EPS)) * g


def _norm_proj_kernel(x_ref, g_ref, w_ref, *rest, segments, head_w, half_w, q_steps, q_scale):
    n_seg = len(segments)
    o_refs, xn_ref, scratch = rest[:n_seg], rest[n_seg], rest[n_seg + 1:]
    j = pl.program_id(2)
    tm, tn = xn_ref.shape[0], w_ref.shape[-1]
    per_half = half_w // head_w

    @pl.when(j == 0)
    def _():
        xn_ref[...] = _rmsnorm_f32(x_ref[0], g_ref[...]).astype(BF16)

    def project(half):
        return jnp.dot(xn_ref[...], w_ref[:, half * half_w:(half + 1) * half_w], preferred_element_type=F32)

    def deinterleave(o_ref, h, dil):
        stage_ref = scratch[0]
        if dil <= DIL_RATIO:
            for r in range(dil):
                o_ref[0, h, r] = stage_ref[h, pl.ds(r, tm // dil, stride=dil), :].astype(o_ref.dtype)
            return
        stage2_ref = scratch[1]
        outer = dil // DIL_RATIO
        for r1 in range(DIL_RATIO):
            stage2_ref[h, r1] = stage_ref[h, pl.ds(r1, tm // DIL_RATIO, stride=DIL_RATIO), :]
        for r1 in range(DIL_RATIO):
            for k in range(outer):
                o_ref[0, h, k * DIL_RATIO + r1] = (
                    stage2_ref[h, r1, pl.ds(k, tm // dil, stride=outer), :].astype(o_ref.dtype))

    start = 0
    for o_ref, (steps, dil) in zip(o_refs, segments):
        def segment(o_ref=o_ref, dil=dil, start=start):
            for half in range(tn // half_w):
                res = project(half)
                if q_steps:
                    res = res * jnp.where(j - start < q_steps, q_scale, 1.0)
                for hh in range(per_half):
                    h = half * per_half + hh
                    cols = res[:, hh * head_w:(hh + 1) * head_w]
                    if dil == 1:
                        o_ref[0, h] = cols.astype(o_ref.dtype)
                    else:
                        scratch[0][h] = cols
                if dil > 1:
                    for hh in range(per_half):
                        deinterleave(o_ref, half * per_half + hh, dil)

        if n_seg == 1:
            segment()
        else:
            pl.when((j >= start) & (j < start + steps))(segment)
        start += steps


def _norm_proj(x, g, w, *, head_w, tm, tn=TN, segments=None, q_steps=0, q_scale=1.0):
    b, s, d = x.shape
    nt = w.shape[1] // tn
    hps = tn // head_w
    segments = segments or ((nt, 1),)
    max_dil = max(dil for _, dil in segments)
    assert s % tm == 0 and w.shape == (d, nt * tn) and tn % max(head_w, MXU_COLS) == 0
    assert sum(steps for steps, _ in segments) == nt
    assert tm % (max_dil * 16) == 0
    scratch = [pltpu.VMEM((tm, d), BF16)]
    if max_dil > 1:
        scratch.append(pltpu.VMEM((hps, tm, head_w), F32))
    if max_dil > DIL_RATIO:
        scratch.append(pltpu.VMEM((hps, DIL_RATIO, tm // DIL_RATIO, head_w), F32))
    out_specs, out_shapes, start = [], [], 0
    for steps, dil in segments:
        def held(j, start=start, steps=steps):
            return jnp.clip(j - start, 0, steps - 1)
        if dil == 1:
            out_specs.append(pl.BlockSpec((1, hps, tm, head_w), lambda bi, i, j, held=held: (bi, held(j), i, 0)))
            out_shapes.append(jax.ShapeDtypeStruct((b, steps * hps, s, head_w), BF16))
        else:
            out_specs.append(pl.BlockSpec((1, hps, dil, tm // dil, head_w),
                                          lambda bi, i, j, held=held: (bi, held(j), 0, i, 0)))
            out_shapes.append(jax.ShapeDtypeStruct((b, steps * hps, dil, s // dil, head_w), BF16))
        start += steps
    kern = functools.partial(_norm_proj_kernel, segments=tuple(segments), head_w=head_w,
                             half_w=max(head_w, MXU_COLS), q_steps=q_steps, q_scale=q_scale)
    outs = pl.pallas_call(
        kern,
        grid=(b, s // tm, nt),
        in_specs=[
            pl.BlockSpec((1, tm, d), lambda bi, i, j: (bi, i, 0)),
            pl.BlockSpec((1, d), lambda bi, i, j: (0, 0)),
            pl.BlockSpec((d, tn), lambda bi, i, j: (0, j)),
        ],
        out_specs=out_specs,
        out_shape=out_shapes,
        scratch_shapes=scratch,
        compiler_params=_params(("parallel", "parallel", "arbitrary")),
        name="norm_proj",
    )(x, g.reshape(1, d), w)
    return outs if len(outs) > 1 else outs[0]


def _dil_attn_kernel(q0, k0, v0, q1, k1, v1, q2, k2, v2, o_ref,
                     m1_ref, l1_ref, a1_ref, m0_ref, l0_ref, a0_ref, bias0_ref, bias1_ref, bias2_ref,
                     *, seq, unroll):
    h = pl.program_id(0)
    groups = ((q0, k0, v0, bias0_ref), (q1, k1, v1, bias1_ref), (q2, k2, v2, bias2_ref))
    stat = (ATT_BLOCK, HEAD_DIM)

    def build_bias(g):
        bias_ref = groups[g][3]
        dil = DIL_GROUPS[g][1]
        kw = bias_ref.shape[-1]
        jf = (h + (g * A_SLOTS + 1)).astype(F32)
        slope = jnp.exp2(jnp.full((ATT_BLOCK, kw), -8.0 / A_HEADS, F32) * jf)
        row = lax.broadcasted_iota(jnp.int32, (ATT_BLOCK, kw), 0)
        col = lax.broadcasted_iota(jnp.int32, (ATT_BLOCK, kw), 1)
        for t, delta in enumerate((0, -ATT_HALF, ATT_BLOCK - kw)):
            arel = jnp.abs(col - row + delta)
            bias_ref[t] = jnp.where(arel <= ATT_HALF, (-slope * (dil * arel).astype(F32)) * LOG2_E, NEG_BIG)

    def scores(g, q, k, blk, nblk):
        bias_ref = groups[g][3]
        table = jnp.where(blk == 0, 0, jnp.where(blk == nblk - 1, 2, 1))
        return lax.dot_general(q, k, (((1,), (1,)), ((), ())), preferred_element_type=F32) + bias_ref[table]

    def accumulate(s, v, state):
        m_new = jnp.broadcast_to(jnp.max(s, axis=-1, keepdims=True), stat)
        if state is not None:
            m_old, l_old, acc_old = state
            m_new = jnp.maximum(m_old, m_new)
            alpha = jnp.exp2(m_old - m_new)
        slabs = [s[:, c:c + HEAD_DIM] - m_new for c in range(0, s.shape[1], HEAD_DIM)]
        p = jnp.exp2(slabs[0] if len(slabs) == 1 else jnp.concatenate(slabs, axis=1))
        l_new = jnp.broadcast_to(jnp.sum(p, axis=-1, keepdims=True), stat)
        acc_new = jnp.dot(p.astype(BF16), v, preferred_element_type=F32)
        if state is not None:
            l_new = alpha * l_old + l_new
            acc_new = alpha * acc_old + acc_new
        return m_new, l_new, acc_new

    def run_group(g):
        q_ref, k_ref, v_ref, bias_ref = groups[g]
        dil = DIL_GROUPS[g][1]
        length = seq // dil
        kw = bias_ref.shape[-1]
        nblk = length // ATT_BLOCK
        shift = nblk.bit_length() - 1

        def score_block(it):
            r = it >> shift
            blk = it & (nblk - 1)
            i0 = pl.multiple_of(blk * ATT_BLOCK, ATT_BLOCK)
            ws = pl.multiple_of(jnp.clip(i0 - ATT_HALF, 0, length - kw), ATT_HALF)
            if g == 0:
                q = q_ref[0, 0, pl.ds(i0, ATT_BLOCK), :]
                k = k_ref[0, 0, pl.ds(ws, kw), :]
                v = v_ref[0, 0, pl.ds(ws, kw), :]
            else:
                q = q_ref[0, 0, r, pl.ds(i0, ATT_BLOCK), :]
                k = k_ref[0, 0, r, pl.ds(ws, kw), :]
                v = v_ref[0, 0, r, pl.ds(ws, kw), :]
            return r, i0, scores(g, q, k, blk, nblk), v

        def finish_block(r, i0, s, v):
            if g == 2:
                m, l, acc = accumulate(s, v, None)
                rows = pl.ds(i0 * DIL_RATIO + (r >> 2), ATT_BLOCK, stride=DIL_RATIO)
                m1_ref[r & 3, rows, :] = m
                l1_ref[r & 3, rows, :] = l
                a1_ref[r & 3, rows, :] = acc
            elif g == 1:
                rows = pl.ds(i0, ATT_BLOCK)
                m, l, acc = accumulate(s, v, (m1_ref[r, rows, :], l1_ref[r, rows, :], a1_ref[r, rows, :]))
                rows = pl.ds(i0 * DIL_RATIO + r, ATT_BLOCK, stride=DIL_RATIO)
                m0_ref[rows, :] = m
                l0_ref[rows, :] = l
                a0_ref[rows, :] = acc
            else:
                rows = pl.ds(i0, ATT_BLOCK)
                _, l, acc = accumulate(s, v, (m0_ref[rows, :], l0_ref[rows, :], a0_ref[rows, :]))
                o_ref[0, rows, :] = (acc / l).astype(o_ref.dtype)

        total = dil * nblk
        batch = min(ATT_BATCH, total) if kw < ATT_KEYS else 1

        def body(bi, carry):
            blocks = [score_block(bi * batch + i) for i in range(batch)]
            for blk_args in blocks:
                finish_block(*blk_args)
            return carry

        lax.fori_loop(0, total // batch, body, 0, unroll=max(1, min(unroll, total) // batch))

    @pl.when(pl.program_id(1) == 0)
    def _():
        for g in range(N_GROUPS):
            build_bias(g)

    run_group(2)
    run_group(1)
    run_group(0)


def _dilated_attention(p0, p1, p2, *, unroll=32):
    b, _, s, hd = p0.shape
    n_far = s // (DIL_GROUPS[-1][1] * ATT_BLOCK)
    assert n_far >= 1 and s % (DIL_GROUPS[-1][1] * ATT_BLOCK) == 0 and n_far & (n_far - 1) == 0
    assert all(window // (2 * dil) == ATT_HALF for window, dil in DIL_GROUPS)
    assert all(DIL_GROUPS[g + 1][1] == DIL_RATIO * DIL_GROUPS[g][1] for g in range(N_GROUPS - 1))
    args, specs = [], []
    for c in range(3):
        args.append(p0)
        specs.append(pl.BlockSpec((1, 1, s, hd), lambda hi, bi, c=c: (bi, c * A_SLOTS + hi, 0, 0)))
    for pg in (p1, p2):
        dil, length = pg.shape[2], pg.shape[3]
        for c in range(3):
            args.append(pg)
            specs.append(pl.BlockSpec((1, 1, dil, length, hd), lambda hi, bi, c=c: (bi, c * A_SLOTS + hi, 0, 0, 0)))
    l1 = s // DIL_GROUPS[1][1]
    bias = [pltpu.VMEM((3, ATT_BLOCK, min(ATT_KEYS, s // dil)), F32) for _, dil in DIL_GROUPS]
    return pl.pallas_call(
        functools.partial(_dil_attn_kernel, seq=s, unroll=unroll),
        grid=(A_SLOTS, b),
        in_specs=specs,
        out_specs=pl.BlockSpec((1, s, hd), lambda hi, bi: (bi, 0, hi)),
        out_shape=jax.ShapeDtypeStruct((b, s, A_SLOTS * hd), BF16),
        scratch_shapes=[pltpu.VMEM((DIL_RATIO, l1, hd), F32)] * 3 + [pltpu.VMEM((s, hd), F32)] * 3 + bias,
        compiler_params=_params(("parallel", "arbitrary")),
        name="dilated_attention",
    )(*args)


def _retention_kernel(dec_ref, q_ref, k_ref, v_ref, gate_ref, o_ref,
                      yf_ref, yb_ref, sf_ref, sb_ref, qdf_ref, kdf_ref, qdb_ref, kdb_ref, intra_ref,
                      *, seq, unroll):
    h = pl.program_id(0)
    c = RET_CHUNK
    n = seq // c
    dk = RET_DK
    q_scale = dk ** -0.5

    def log_gamma(e, shape):
        return jnp.log1p(-jnp.exp2(-jnp.full(shape, e, F32)))

    e_f = dec_ref[0, h]
    e_b = dec_ref[1, h]
    cd_f = jnp.exp(c * log_gamma(e_f, (1, dk)))
    cd_b = jnp.exp(c * log_gamma(e_b, (1, dk)))

    @pl.when(pl.program_id(1) == 0)
    def _():
        lg_f = log_gamma(e_f, (c, dk))
        lg_b = log_gamma(e_b, (c, dk))
        i_row = lax.broadcasted_iota(jnp.int32, (c, dk), 0).astype(F32)
        qdf_ref[...] = jnp.exp((i_row + 1.0) * lg_f) * q_scale
        kdf_ref[...] = jnp.exp((c - 1.0 - i_row) * lg_f)
        qdb_ref[...] = jnp.exp((c - i_row) * lg_b) * q_scale
        kdb_ref[...] = jnp.exp(i_row * lg_b)
        t = lax.broadcasted_iota(jnp.int32, (c, c), 0)
        s = lax.broadcasted_iota(jnp.int32, (c, c), 1)
        diff = (t - s).astype(F32)
        intra_ref[...] = q_scale * jnp.where(
            t >= s, jnp.exp(jnp.where(t >= s, diff, 0.0) * log_gamma(e_f, (c, c))),
            jnp.exp(jnp.where(t < s, -diff, 0.0) * log_gamma(e_b, (c, c))))

    def load(ci):
        rows = pl.ds(pl.multiple_of(ci * c, c), c)
        q = q_ref[0, 0, rows, :].astype(F32)
        k = k_ref[0, 0, rows, :].astype(F32)
        v = v_ref[0, 0, rows, :]
        return rows, q, k, v

    def inter(q_dec, k_dec, v, st_ref, cd):
        y = jnp.dot(q_dec.astype(BF16), st_ref[...].astype(BF16), preferred_element_type=F32)
        kv = lax.dot_general(k_dec.astype(BF16), v, (((0,), (0,)), ((), ())), preferred_element_type=F32)
        st_ref[...] = cd * st_ref[...] + kv
        return y

    sf_ref[...] = jnp.zeros_like(sf_ref)
    sb_ref[...] = jnp.zeros_like(sb_ref)

    def finish(rows, y):
        y = y * lax.rsqrt(jnp.mean(y * y, axis=-1, keepdims=True) + EPS)
        gate = gate_ref[0, 0, rows, :].astype(F32)
        o_ref[0, rows, :] = (y * (gate * jax.nn.sigmoid(gate))).astype(o_ref.dtype)

    half = n // 2

    def step(ci, carry, *, second_half):
        cj = n - 1 - ci
        rows, q, k, v = load(ci)
        sc = lax.dot_general(q_ref[0, 0, rows, :], k_ref[0, 0, rows, :], (((1,), (1,)), ((), ())),
                             preferred_element_type=F32) * intra_ref[...]
        y_f = jnp.dot(sc.astype(BF16), v, preferred_element_type=F32)
        y_f = y_f + inter(q * qdf_ref[...], k * kdf_ref[...], v, sf_ref, cd_f)
        if second_half:
            finish(rows, y_f + yb_ref[pl.ds(pl.multiple_of((ci - half) * c, c), c), :])
        else:
            yf_ref[rows, :] = y_f
        rows, q, k, v = load(cj)
        y_b = inter(q * qdb_ref[...], k * kdb_ref[...], v, sb_ref, cd_b)
        if second_half:
            finish(rows, yf_ref[rows, :] + y_b)
        else:
            yb_ref[pl.ds(pl.multiple_of((cj - half) * c, c), c), :] = y_b
        return carry

    lax.fori_loop(0, half, functools.partial(step, second_half=False), 0, unroll=min(unroll, half))
    lax.fori_loop(half, n, functools.partial(step, second_half=True), 0, unroll=min(unroll, half))


def _retention(proj, decays, *, unroll=8):
    b, _, s, dk = proj.shape
    c = RET_CHUNK
    assert dk == RET_DK and s % (2 * c) == 0

    def head_spec(offset):
        return pl.BlockSpec((1, 1, s, dk), lambda hi, bi: (bi, offset + hi, 0, 0))

    return pl.pallas_call(
        functools.partial(_retention_kernel, seq=s, unroll=unroll),
        grid=(RET_HEADS, b),
        in_specs=[pl.BlockSpec(memory_space=pltpu.SMEM)] + [head_spec(RET_HEADS * i) for i in range(4)],
        out_specs=pl.BlockSpec((1, s, dk), lambda hi, bi: (bi, 0, hi)),
        out_shape=jax.ShapeDtypeStruct((b, s, RET_HEADS * dk), BF16),
        scratch_shapes=[pltpu.VMEM((s // 2, dk), F32)] * 2 + [pltpu.VMEM((dk, dk), F32)] * 2
        + [pltpu.VMEM((c, dk), F32)] * 4 + [pltpu.VMEM((c, c), F32)],
        compiler_params=_params(("parallel", "arbitrary")),
        name="retention",
    )(decays, proj, proj, proj, proj)


def _out_proj_kernel(x_ref, mix_ref, q_ref, kv_ref, w1_ref, w2_ref, o_ref, *, q_w):
    scale = HEAD_DIM ** -0.5
    per = q_w // HEAD_DIM
    scores = []
    for hh in range(CROSS_HEADS):
        q = q_ref[0, hh // per][:, (hh % per) * HEAD_DIM:(hh % per + 1) * HEAD_DIM]
        scores.append(lax.dot_general(q, kv_ref[0, hh], (((1,), (1,)), ((), ())),
                                      preferred_element_type=F32) * scale)
    acc = jnp.dot(mix_ref[0], w1_ref[...], preferred_element_type=F32)
    outs = []
    for hh, s in enumerate(scores):
        m = jnp.max(s, axis=-1, keepdims=True)
        p = jnp.exp(s - m)
        l = jnp.sum(p, axis=-1, keepdims=True)
        o = jnp.dot(p.astype(BF16), kv_ref[0, CROSS_HEADS + hh], preferred_element_type=F32) / l
        outs.append(o.astype(BF16))
    cross = jnp.concatenate(outs, axis=1)
    acc = acc + jnp.dot(cross, w2_ref[...], preferred_element_type=F32)
    o_ref[0] = x_ref[0] + acc


def _out_proj(x, mix, proj, q_head0, kv, w1, w2, *, tm):
    b, s, d = x.shape
    km = mix.shape[-1]
    q_w = proj.shape[-1]
    nq = CROSS_DIM // q_w
    assert s % tm == 0 and q_head0 % nq == 0 and w1.shape == (km, d) and w2.shape == (CROSS_DIM, d)
    return pl.pallas_call(
        functools.partial(_out_proj_kernel, q_w=q_w),
        grid=(b, s // tm),
        in_specs=[
            pl.BlockSpec((1, tm, d), lambda bi, i: (bi, i, 0)),
            pl.BlockSpec((1, tm, km), lambda bi, i: (bi, i, 0)),
            pl.BlockSpec((1, nq, tm, q_w), lambda bi, i: (bi, q_head0 // nq, i, 0)),
            pl.BlockSpec((1, 2 * CROSS_HEADS, N_MEM, HEAD_DIM), lambda bi, i: (0, 0, bi, 0)),
            pl.BlockSpec((km, d), lambda bi, i: (0, 0)),
            pl.BlockSpec((CROSS_DIM, d), lambda bi, i: (0, 0)),
        ],
        out_specs=pl.BlockSpec((1, tm, d), lambda bi, i: (bi, i, 0)),
        out_shape=jax.ShapeDtypeStruct((b, s, d), F32),
        compiler_params=_params(("parallel", "parallel")),
        name="out_proj",
    )(x, mix, proj, kv, w1, w2)


def _ffn_kernel(x_ref, g_ref, gf_ref, wg_ref, wu_ref, wd_ref, o_ref, xn_ref, *, final_norm):
    f = pl.program_id(2)

    @pl.when(f == 0)
    def _():
        x = x_ref[0]
        xn_ref[...] = _rmsnorm_f32(x, g_ref[...]).astype(BF16)
        o_ref[0] = x

    xn = xn_ref[...]
    acts = []
    for c in range(0, wg_ref.shape[1], MXU_COLS):
        gate = jnp.dot(xn, wg_ref[:, c:c + MXU_COLS], preferred_element_type=F32)
        up = jnp.dot(xn, wu_ref[:, c:c + MXU_COLS], preferred_element_type=F32)
        acts.append(((gate * jax.nn.sigmoid(gate)) * up).astype(BF16))
    o_ref[0] += jnp.dot(jnp.concatenate(acts, axis=1), wd_ref[...], preferred_element_type=F32)

    if final_norm:
        @pl.when(f == pl.num_programs(2) - 1)
        def _():
            o_ref[0] = _rmsnorm_f32(o_ref[0], gf_ref[...])


def _ffn(x, g, g_final, w_gate_up, w_down, *, final_norm, tm, tf=TN):
    b, s, d = x.shape
    nf = w_down.shape[0] // tf
    assert s % tm == 0 and w_down.shape == (nf * tf, d) and w_gate_up.shape == (d, 2 * nf * tf)
    return pl.pallas_call(
        functools.partial(_ffn_kernel, final_norm=final_norm),
        grid=(b, s // tm, nf),
        in_specs=[
            pl.BlockSpec((1, tm, d), lambda bi, i, f: (bi, i, 0)),
            pl.BlockSpec((1, d), lambda bi, i, f: (0, 0)),
            pl.BlockSpec((1, d), lambda bi, i, f: (0, 0)),
            pl.BlockSpec((d, tf), lambda bi, i, f: (0, f)),
            pl.BlockSpec((d, tf), lambda bi, i, f: (0, nf + f)),
            pl.BlockSpec((tf, d), lambda bi, i, f: (f, 0)),
        ],
        out_specs=pl.BlockSpec((1, tm, d), lambda bi, i, f: (bi, i, 0)),
        out_shape=jax.ShapeDtypeStruct((b, s, d), F32),
        scratch_shapes=[pltpu.VMEM((tm, d), BF16)],
        compiler_params=_params(("parallel", "parallel", "arbitrary")),
        name="ffn",
    )(x, g.reshape(1, d), g_final.reshape(1, d), w_gate_up, w_gate_up, w_down)


def _trunk(x, mem, p):
    for i in range(2):
        kv = _norm_proj(mem.reshape(1, -1, mem.shape[-1]), p["norm_mem"][i], p["w_mem_kv"][i],
                        head_w=HEAD_DIM, tm=mem.shape[0] * N_MEM)
        if i == 0:
            projs = _norm_proj(x, p["norm_mix"][i], p["a_w_in"], head_w=HEAD_DIM, tm=TM_PROJ, tn=TN_A,
                               segments=p["a_segments"], q_steps=A_SLOTS * HEAD_DIM // TN_A, q_scale=ATT_Q_SCALE)
            mix = _dilated_attention(*projs)
            proj, q_head0 = projs[0], 3 * A_SLOTS
            w1, w2 = p["a_w_out1"], p["a_w_out2"]
        else:
            proj, q_head0 = _norm_proj(x, p["norm_mix"][i], p["b_w_in"], head_w=RET_DK, tm=TM_PROJ_B), 4 * RET_HEADS
            mix = _retention(proj, p["b_decay"])
            w1, w2 = p["b_w_out1"], p["b_w_out2"]
        x = _out_proj(x, mix, proj, q_head0, kv, w1, w2, tm=TM_OUT)
        x = _ffn(x, p["norm_ffn"][i], p["norm_final"], p["w_gate_up"][i], p["w_down"][i],
                 final_norm=(i == 1), tm=TM_FFN)
    return x


def _group_columns(w_in):
    gw = A_SLOTS * HEAD_DIM
    cols, segments = [], []
    for g, (_, dil) in enumerate(DIL_GROUPS):
        group = [w_in[:, (c * N_GROUPS + g) * gw:(c * N_GROUPS + g + 1) * gw] for c in range(3)]
        if g == 0:
            group.append(w_in[:, 3 * N_GROUPS * gw:])
        width = sum(w.shape[1] for w in group)
        pad = -width % TN_A
        if pad:
            group.append(jnp.zeros((w_in.shape[0], pad), w_in.dtype))
        cols += group
        segments.append(((width + pad) // TN_A, dil))
    return jnp.concatenate(cols, axis=1), tuple(segments)


def kernel(x_prompt, x_sample, mem_prompt, mem_sample, norm_mix, norm_mem, w_mem_kv, a_w_in, a_w_out,
           b_w_in, b_w_out, b_decay_fwd, b_decay_bwd, norm_ffn, w_gate_up, w_down, norm_final):
    a_mix = A_SLOTS * HEAD_DIM
    b_mix = RET_HEADS * RET_DK
    assert x_prompt.shape[-1] == D_MODEL and x_sample.shape[-1] == D_MODEL
    assert mem_prompt.shape[1:] == (N_MEM, D_MODEL) and mem_sample.shape[1:] == (N_MEM, D_MODEL)
    assert norm_mix.shape[0] == 2 and a_w_in.shape[0] == 1 and b_w_in.shape[0] == 1
    a_cols, a_segments = _group_columns(a_w_in[0])
    p = {
        "norm_mix": norm_mix, "norm_mem": norm_mem, "norm_ffn": norm_ffn, "norm_final": norm_final,
        "w_mem_kv": w_mem_kv.astype(BF16),
        "a_w_in": a_cols.astype(BF16), "a_segments": a_segments,
        "a_w_out1": a_w_out[0, :a_mix].astype(BF16), "a_w_out2": a_w_out[0, a_mix:].astype(BF16),
        "b_w_in": b_w_in[0].astype(BF16),
        "b_w_out1": b_w_out[0, :b_mix].astype(BF16), "b_w_out2": b_w_out[0, b_mix:].astype(BF16),
        "b_decay": jnp.stack([b_decay_fwd[0], b_decay_bwd[0]]).astype(F32),
        "w_gate_up": w_gate_up.astype(BF16), "w_down": w_down.astype(BF16),
    }
    return _trunk(x_prompt, mem_prompt, p), _trunk(x_sample, mem_sample, p)
```

```python
import functools

import jax
import jax.numpy as jnp
from jax import lax
from jax.experimental import pallas as pl
from jax.experimental.pallas import tpu as pltpu

F32 = jnp.float32
BF16 = jnp.bfloat16

D_MODEL = 2048
HEAD_DIM = 128
DIL_GROUPS = ((128, 1), (512, 4), (2048, 16))
N_GROUPS = len(DIL_GROUPS)
A_SLOTS = D_MODEL // 256
A_HEADS = N_GROUPS * A_SLOTS
RET_HEADS = D_MODEL // 256
RET_DK = 256
RET_CHUNK = 256
CROSS_HEADS = 4
CROSS_DIM = CROSS_HEADS * HEAD_DIM
N_MEM = 256
D_FF = 5632
EPS = 1e-6
NEG_BIG = -1e30
LOG2_E = 1.4426950408889634
ATT_Q_SCALE = HEAD_DIM ** -0.5 * LOG2_E

ATT_BLOCK = 128
ATT_HALF = 64
ATT_KEYS = 256
ATT_BATCH = 8
DIL_RATIO = 4

TM_PROJ = 1024
TM_PROJ_B = 2048
TM_OUT = 512
TM_FFN = 1024
TN = 512
TN_A = 1024

V7X_VMEM_LIMIT = 56 * 1024 * 1024
PROJ_B_VMEM_LIMIT = 60 * 1024 * 1024
MXU_COLS = 256


def _params(semantics):
    return pltpu.CompilerParams(dimension_semantics=semantics, vmem_limit_bytes=V7X_VMEM_LIMIT)


def _rmsnorm_f32(x, g):
    ms = jnp.mean(x * x, axis=-1, keepdims=True)
    return (x * lax.rsqrt(ms + EPS)) * g


def _norm_proj_kernel(x_ref, g_ref, w_ref, *rest, segments, head_w, half_w, q_steps, q_scale, hold):
    n_seg = len(segments)
    o_refs, xn_ref, scratch = rest[:n_seg], rest[n_seg], rest[n_seg + 1:]
    j = pl.program_id(2)
    tm, tn = xn_ref.shape[0], w_ref.shape[-1]
    per_half = half_w // head_w

    @pl.when(j == 0)
    def _():
        xn_ref[...] = _rmsnorm_f32(x_ref[0], g_ref[...]).astype(BF16)

    def project(half):
        return jnp.dot(xn_ref[...], w_ref[:, half * half_w:(half + 1) * half_w], preferred_element_type=F32)

    def deinterleave(o_ref, h, dil):
        stage_ref = scratch[0]
        if dil <= DIL_RATIO:
            for r in range(dil):
                o_ref[0, h, r] = stage_ref[h, pl.ds(r, tm // dil, stride=dil), :].astype(o_ref.dtype)
            return
        stage2_ref = scratch[1]
        outer = dil // DIL_RATIO
        for r1 in range(DIL_RATIO):
            stage2_ref[h, r1] = stage_ref[h, pl.ds(r1, tm // DIL_RATIO, stride=DIL_RATIO), :]
        for r1 in range(DIL_RATIO):
            for k in range(outer):
                o_ref[0, h, k * DIL_RATIO + r1] = (
                    stage2_ref[h, r1, pl.ds(k, tm // dil, stride=outer), :].astype(o_ref.dtype))

    start = 0
    for o_ref, (steps, dil) in zip(o_refs, segments):
        def segment(o_ref=o_ref, dil=dil, start=start):
            for half in range(tn // half_w):
                res = project(half)
                if q_steps:
                    res = res * jnp.where(j - start < q_steps, q_scale, 1.0)
                for hh in range(per_half):
                    h = half * per_half + hh
                    cols = res[:, hh * head_w:(hh + 1) * head_w]
                    if dil == 1:
                        slot = ((j - start) % hold) * (tn // head_w) + h if hold > 1 else h
                        o_ref[0, slot] = cols.astype(o_ref.dtype)
                    else:
                        scratch[0][h] = cols
                if dil > 1:
                    for hh in range(per_half):
                        deinterleave(o_ref, half * per_half + hh, dil)

        if n_seg == 1:
            segment()
        else:
            pl.when((j >= start) & (j < start + steps))(segment)
        start += steps


def _norm_proj(x, g, w, *, head_w, tm, tn=TN, segments=None, q_steps=0, q_scale=1.0, hold=1,
               vmem_limit=V7X_VMEM_LIMIT):
    b, s, d = x.shape
    nt = w.shape[1] // tn
    hps = tn // head_w
    segments = segments or ((nt, 1),)
    max_dil = max(dil for _, dil in segments)
    assert s % tm == 0 and w.shape == (d, nt * tn) and tn % max(head_w, MXU_COLS) == 0
    assert sum(steps for steps, _ in segments) == nt
    assert tm % (max_dil * 16) == 0
    scratch = [pltpu.VMEM((tm, d), BF16)]
    if max_dil > 1:
        scratch.append(pltpu.VMEM((hps, tm, head_w), F32))
    if max_dil > DIL_RATIO:
        scratch.append(pltpu.VMEM((hps, DIL_RATIO, tm // DIL_RATIO, head_w), F32))
    out_specs, out_shapes, start = [], [], 0
    for steps, dil in segments:
        def held(j, start=start, steps=steps):
            return jnp.clip(j - start, 0, steps - 1)
        if dil == 1:
            out_specs.append(pl.BlockSpec((1, hps * hold, tm, head_w),
                                          lambda bi, i, j, held=held: (bi, held(j) // hold, i, 0)))
            out_shapes.append(jax.ShapeDtypeStruct((b, steps * hps, s, head_w), BF16))
        else:
            out_specs.append(pl.BlockSpec((1, hps, dil, tm // dil, head_w),
                                          lambda bi, i, j, held=held: (bi, held(j), 0, i, 0)))
            out_shapes.append(jax.ShapeDtypeStruct((b, steps * hps, dil, s // dil, head_w), BF16))
        start += steps
    kern = functools.partial(_norm_proj_kernel, segments=tuple(segments), head_w=head_w,
                             half_w=max(head_w, MXU_COLS), q_steps=q_steps, q_scale=q_scale, hold=hold)
    outs = pl.pallas_call(
        kern,
        grid=(b, s // tm, nt),
        in_specs=[
            pl.BlockSpec((1, tm, d), lambda bi, i, j: (bi, i, 0)),
            pl.BlockSpec((1, d), lambda bi, i, j: (0, 0)),
            pl.BlockSpec((d, tn), lambda bi, i, j: (0, j)),
        ],
        out_specs=out_specs,
        out_shape=out_shapes,
        scratch_shapes=scratch,
        compiler_params=pltpu.CompilerParams(dimension_semantics=("parallel", "parallel", "arbitrary"),
                                             vmem_limit_bytes=vmem_limit),
        name="norm_proj",
    )(x, g.reshape(1, d), w)
    return outs if len(outs) > 1 else outs[0]


def _dil_attn_kernel(q0, k0, v0, q1, k1, v1, q2, k2, v2, o_ref,
                     m1_ref, l1_ref, a1_ref, m0_ref, l0_ref, a0_ref, bias0_ref, bias1_ref, bias2_ref,
                     *, seq, unroll):
    h = pl.program_id(0)
    groups = ((q0, k0, v0, bias0_ref), (q1, k1, v1, bias1_ref), (q2, k2, v2, bias2_ref))
    stat = (ATT_BLOCK, HEAD_DIM)

    def build_bias(g):
        bias_ref = groups[g][3]
        dil = DIL_GROUPS[g][1]
        kw = bias_ref.shape[-1]
        jf = (h + (g * A_SLOTS + 1)).astype(F32)
        slope = jnp.exp2(jnp.full((ATT_BLOCK, kw), -8.0 / A_HEADS, F32) * jf)
        row = lax.broadcasted_iota(jnp.int32, (ATT_BLOCK, kw), 0)
        col = lax.broadcasted_iota(jnp.int32, (ATT_BLOCK, kw), 1)
        for t, delta in enumerate((0, -ATT_HALF, ATT_BLOCK - kw)):
            arel = jnp.abs(col - row + delta)
            bias_ref[t] = jnp.where(arel <= ATT_HALF, (-slope * (dil * arel).astype(F32)) * LOG2_E, NEG_BIG)

    def scores(g, q, k, blk, nblk):
        bias_ref = groups[g][3]
        table = jnp.where(blk == 0, 0, jnp.where(blk == nblk - 1, 2, 1))
        return lax.dot_general(q, k, (((1,), (1,)), ((), ())), preferred_element_type=F32) + bias_ref[table]

    def accumulate(s, v, state):
        m_new = jnp.broadcast_to(jnp.max(s, axis=-1, keepdims=True), stat)
        if state is not None:
            m_old, l_old, acc_old = state
            m_new = jnp.maximum(m_old, m_new)
            alpha = jnp.exp2(m_old - m_new)
        slabs = [s[:, c:c + HEAD_DIM] - m_new for c in range(0, s.shape[1], HEAD_DIM)]
        p = jnp.exp2(slabs[0] if len(slabs) == 1 else jnp.concatenate(slabs, axis=1))
        l_new = jnp.broadcast_to(jnp.sum(p, axis=-1, keepdims=True), stat)
        acc_new = jnp.dot(p.astype(BF16), v, preferred_element_type=F32)
        if state is not None:
            l_new = alpha * l_old + l_new
            acc_new = alpha * acc_old + acc_new
        return m_new, l_new, acc_new

    def run_group(g):
        q_ref, k_ref, v_ref, bias_ref = groups[g]
        dil = DIL_GROUPS[g][1]
        length = seq // dil
        kw = bias_ref.shape[-1]
        nblk = length // ATT_BLOCK
        shift = nblk.bit_length() - 1

        def score_block(it):
            r = it >> shift
            blk = it & (nblk - 1)
            i0 = pl.multiple_of(blk * ATT_BLOCK, ATT_BLOCK)
            ws = pl.multiple_of(jnp.clip(i0 - ATT_HALF, 0, length - kw), ATT_HALF)
            if g == 0:
                q = q_ref[0, 0, pl.ds(i0, ATT_BLOCK), :]
                k = k_ref[0, 0, pl.ds(ws, kw), :]
                v = v_ref[0, 0, pl.ds(ws, kw), :]
            else:
                q = q_ref[0, 0, r, pl.ds(i0, ATT_BLOCK), :]
                k = k_ref[0, 0, r, pl.ds(ws, kw), :]
                v = v_ref[0, 0, r, pl.ds(ws, kw), :]
            return r, i0, scores(g, q, k, blk, nblk), v

        def finish_block(r, i0, s, v):
            if g == 2:
                m, l, acc = accumulate(s, v, None)
                rows = pl.ds(i0 * DIL_RATIO + (r >> 2), ATT_BLOCK, stride=DIL_RATIO)
                m1_ref[r & 3, rows, :] = m
                l1_ref[r & 3, rows, :] = l
                a1_ref[r & 3, rows, :] = acc
            elif g == 1:
                rows = pl.ds(i0, ATT_BLOCK)
                m, l, acc = accumulate(s, v, (m1_ref[r, rows, :], l1_ref[r, rows, :], a1_ref[r, rows, :]))
                rows = pl.ds(i0 * DIL_RATIO + r, ATT_BLOCK, stride=DIL_RATIO)
                m0_ref[rows, :] = m
                l0_ref[rows, :] = l
                a0_ref[rows, :] = acc
            else:
                rows = pl.ds(i0, ATT_BLOCK)
                _, l, acc = accumulate(s, v, (m0_ref[rows, :], l0_ref[rows, :], a0_ref[rows, :]))
                o_ref[0, rows, :] = (acc / l).astype(o_ref.dtype)

        total = dil * nblk
        batch = min(ATT_BATCH, total) if kw < ATT_KEYS else 1

        def body(bi, carry):
            blocks = [score_block(bi * batch + i) for i in range(batch)]
            for blk_args in blocks:
                finish_block(*blk_args)
            return carry

        lax.fori_loop(0, total // batch, body, 0, unroll=max(1, min(unroll, total) // batch))

    @pl.when(pl.program_id(1) == 0)
    def _():
        for g in range(N_GROUPS):
            build_bias(g)

    run_group(2)
    run_group(1)
    run_group(0)


def _dilated_attention(p0, p1, p2, *, unroll=32):
    b, _, s, hd = p0.shape
    n_far = s // (DIL_GROUPS[-1][1] * ATT_BLOCK)
    assert n_far >= 1 and s % (DIL_GROUPS[-1][1] * ATT_BLOCK) == 0 and n_far & (n_far - 1) == 0
    assert all(window // (2 * dil) == ATT_HALF for window, dil in DIL_GROUPS)
    assert all(DIL_GROUPS[g + 1][1] == DIL_RATIO * DIL_GROUPS[g][1] for g in range(N_GROUPS - 1))
    args, specs = [], []
    for c in range(3):
        args.append(p0)
        specs.append(pl.BlockSpec((1, 1, s, hd), lambda hi, bi, c=c: (bi, c * A_SLOTS + hi, 0, 0)))
    for pg in (p1, p2):
        dil, length = pg.shape[2], pg.shape[3]
        for c in range(3):
            args.append(pg)
            specs.append(pl.BlockSpec((1, 1, dil, length, hd), lambda hi, bi, c=c: (bi, c * A_SLOTS + hi, 0, 0, 0)))
    l1 = s // DIL_GROUPS[1][1]
    bias = [pltpu.VMEM((3, ATT_BLOCK, min(ATT_KEYS, s // dil)), F32) for _, dil in DIL_GROUPS]
    return pl.pallas_call(
        functools.partial(_dil_attn_kernel, seq=s, unroll=unroll),
        grid=(A_SLOTS, b),
        in_specs=specs,
        out_specs=pl.BlockSpec((1, s, hd), lambda hi, bi: (bi, 0, hi)),
        out_shape=jax.ShapeDtypeStruct((b, s, A_SLOTS * hd), BF16),
        scratch_shapes=[pltpu.VMEM((DIL_RATIO, l1, hd), F32)] * 3 + [pltpu.VMEM((s, hd), F32)] * 3 + bias,
        compiler_params=_params(("parallel", "arbitrary")),
        name="dilated_attention",
    )(*args)


def _retention_kernel(dec_ref, q_ref, k_ref, v_ref, gate_ref, o_ref,
                      yf_ref, yb_ref, sf_ref, sb_ref, qdf_ref, kdf_ref, qdb_ref, kdb_ref, intra_ref,
                      *, seq, unroll):
    h = pl.program_id(0)
    c = RET_CHUNK
    n = seq // c
    dk = RET_DK
    q_scale = dk ** -0.5

    def log_gamma(e, shape):
        return jnp.log1p(-jnp.exp2(-jnp.full(shape, e, F32)))

    e_f = dec_ref[0, h]
    e_b = dec_ref[1, h]
    cd_f = jnp.exp(c * log_gamma(e_f, (1, dk)))
    cd_b = jnp.exp(c * log_gamma(e_b, (1, dk)))

    @pl.when(pl.program_id(1) == 0)
    def _():
        lg_f = log_gamma(e_f, (c, dk))
        lg_b = log_gamma(e_b, (c, dk))
        i_row = lax.broadcasted_iota(jnp.int32, (c, dk), 0).astype(F32)
        qdf_ref[...] = jnp.exp((i_row + 1.0) * lg_f) * q_scale
        kdf_ref[...] = jnp.exp((c - 1.0 - i_row) * lg_f)
        qdb_ref[...] = jnp.exp((c - i_row) * lg_b) * q_scale
        kdb_ref[...] = jnp.exp(i_row * lg_b)
        t = lax.broadcasted_iota(jnp.int32, (c, c), 0)
        s = lax.broadcasted_iota(jnp.int32, (c, c), 1)
        diff = (t - s).astype(F32)
        intra_ref[...] = q_scale * jnp.where(
            t >= s, jnp.exp(jnp.where(t >= s, diff, 0.0) * log_gamma(e_f, (c, c))),
            jnp.exp(jnp.where(t < s, -diff, 0.0) * log_gamma(e_b, (c, c))))

    def load(ci):
        rows = pl.ds(pl.multiple_of(ci * c, c), c)
        q = q_ref[0, 0, rows, :].astype(F32)
        k = k_ref[0, 0, rows, :].astype(F32)
        v = v_ref[0, 0, rows, :]
        return rows, q, k, v

    def inter(q_dec, k_dec, v, st_ref, cd):
        y = jnp.dot(q_dec.astype(BF16), st_ref[...].astype(BF16), preferred_element_type=F32)
        kv = lax.dot_general(k_dec.astype(BF16), v, (((0,), (0,)), ((), ())), preferred_element_type=F32)
        st_ref[...] = cd * st_ref[...] + kv
        return y

    sf_ref[...] = jnp.zeros_like(sf_ref)
    sb_ref[...] = jnp.zeros_like(sb_ref)

    def finish(rows, y):
        y = y * lax.rsqrt(jnp.mean(y * y, axis=-1, keepdims=True) + EPS)
        gate = gate_ref[0, 0, rows, :].astype(F32)
        o_ref[0, rows, :] = (y * (gate * jax.nn.sigmoid(gate))).astype(o_ref.dtype)

    half = n // 2

    def step(ci, carry, *, second_half):
        cj = n - 1 - ci
        rows, q, k, v = load(ci)
        sc = lax.dot_general(q_ref[0, 0, rows, :], k_ref[0, 0, rows, :], (((1,), (1,)), ((), ())),
                             preferred_element_type=F32) * intra_ref[...]
        y_f = jnp.dot(sc.astype(BF16), v, preferred_element_type=F32)
        y_f = y_f + inter(q * qdf_ref[...], k * kdf_ref[...], v, sf_ref, cd_f)
        if second_half:
            finish(rows, y_f + yb_ref[pl.ds(pl.multiple_of((ci - half) * c, c), c), :])
        else:
            yf_ref[rows, :] = y_f
        rows, q, k, v = load(cj)
        y_b = inter(q * qdb_ref[...], k * kdb_ref[...], v, sb_ref, cd_b)
        if second_half:
            finish(rows, yf_ref[rows, :] + y_b)
        else:
            yb_ref[pl.ds(pl.multiple_of((cj - half) * c, c), c), :] = y_b
        return carry

    lax.fori_loop(0, half, functools.partial(step, second_half=False), 0, unroll=min(unroll, half))
    lax.fori_loop(half, n, functools.partial(step, second_half=True), 0, unroll=min(unroll, half))


def _retention(proj, decays, *, unroll=8):
    b, _, s, dk = proj.shape
    c = RET_CHUNK
    assert dk == RET_DK and s % (2 * c) == 0

    def head_spec(offset):
        return pl.BlockSpec((1, 1, s, dk), lambda hi, bi: (bi, offset + hi, 0, 0))

    return pl.pallas_call(
        functools.partial(_retention_kernel, seq=s, unroll=unroll),
        grid=(RET_HEADS, b),
        in_specs=[pl.BlockSpec(memory_space=pltpu.SMEM)] + [head_spec(RET_HEADS * i) for i in range(4)],
        out_specs=pl.BlockSpec((1, s, dk), lambda hi, bi: (bi, 0, hi)),
        out_shape=jax.ShapeDtypeStruct((b, s, RET_HEADS * dk), BF16),
        scratch_shapes=[pltpu.VMEM((s // 2, dk), F32)] * 2 + [pltpu.VMEM((dk, dk), F32)] * 2
        + [pltpu.VMEM((c, dk), F32)] * 4 + [pltpu.VMEM((c, c), F32)],
        compiler_params=_params(("parallel", "arbitrary")),
        name="retention",
    )(decays, proj, proj, proj, proj)


def _out_proj_kernel(x_ref, mix_ref, q_ref, kv_ref, w1_ref, w2_ref, o_ref, *, q_w):
    scale = HEAD_DIM ** -0.5
    per = q_w // HEAD_DIM
    scores = []
    for hh in range(CROSS_HEADS):
        q = q_ref[0, hh // per][:, (hh % per) * HEAD_DIM:(hh % per + 1) * HEAD_DIM]
        scores.append(lax.dot_general(q, kv_ref[0, hh], (((1,), (1,)), ((), ())),
                                      preferred_element_type=F32) * scale)
    acc = jnp.dot(mix_ref[0], w1_ref[...], preferred_element_type=F32)
    outs = []
    for hh, s in enumerate(scores):
        m = jnp.max(s, axis=-1, keepdims=True)
        p = jnp.exp(s - m)
        l = jnp.sum(p, axis=-1, keepdims=True)
        o = jnp.dot(p.astype(BF16), kv_ref[0, CROSS_HEADS + hh], preferred_element_type=F32) / l
        outs.append(o.astype(BF16))
    cross = jnp.concatenate(outs, axis=1)
    acc = acc + jnp.dot(cross, w2_ref[...], preferred_element_type=F32)
    o_ref[0] = x_ref[0] + acc


def _out_proj(x, mix, proj, q_head0, kv, w1, w2, *, tm):
    b, s, d = x.shape
    km = mix.shape[-1]
    q_w = proj.shape[-1]
    nq = CROSS_DIM // q_w
    assert s % tm == 0 and q_head0 % nq == 0 and w1.shape == (km, d) and w2.shape == (CROSS_DIM, d)
    return pl.pallas_call(
        functools.partial(_out_proj_kernel, q_w=q_w),
        grid=(b, s // tm),
        in_specs=[
            pl.BlockSpec((1, tm, d), lambda bi, i: (bi, i, 0)),
            pl.BlockSpec((1, tm, km), lambda bi, i: (bi, i, 0)),
            pl.BlockSpec((1, nq, tm, q_w), lambda bi, i: (bi, q_head0 // nq, i, 0)),
            pl.BlockSpec((1, 2 * CROSS_HEADS, N_MEM, HEAD_DIM), lambda bi, i: (0, 0, bi, 0)),
            pl.BlockSpec((km, d), lambda bi, i: (0, 0)),
            pl.BlockSpec((CROSS_DIM, d), lambda bi, i: (0, 0)),
        ],
        out_specs=pl.BlockSpec((1, tm, d), lambda bi, i: (bi, i, 0)),
        out_shape=jax.ShapeDtypeStruct((b, s, d), F32),
        compiler_params=_params(("parallel", "parallel")),
        name="out_proj",
    )(x, mix, proj, kv, w1, w2)


def _ffn_kernel(x_ref, g_ref, gf_ref, wg_ref, wu_ref, wd_ref, o_ref, xn_ref, *, final_norm):
    f = pl.program_id(2)

    @pl.when(f == 0)
    def _():
        x = x_ref[0]
        xn_ref[...] = _rmsnorm_f32(x, g_ref[...]).astype(BF16)
        o_ref[0] = x

    xn = xn_ref[...]
    acts = []
    for c in range(0, wg_ref.shape[1], MXU_COLS):
        gate = jnp.dot(xn, wg_ref[:, c:c + MXU_COLS], preferred_element_type=F32)
        up = jnp.dot(xn, wu_ref[:, c:c + MXU_COLS], preferred_element_type=F32)
        acts.append(((gate * jax.nn.sigmoid(gate)) * up).astype(BF16))
    o_ref[0] += jnp.dot(jnp.concatenate(acts, axis=1), wd_ref[...], preferred_element_type=F32)

    if final_norm:
        @pl.when(f == pl.num_programs(2) - 1)
        def _():
            o_ref[0] = _rmsnorm_f32(o_ref[0], gf_ref[...])


def _ffn(x, g, g_final, w_gate_up, w_down, *, final_norm, tm, tf=TN):
    b, s, d = x.shape
    nf = w_down.shape[0] // tf
    assert s % tm == 0 and w_down.shape == (nf * tf, d) and w_gate_up.shape == (d, 2 * nf * tf)
    return pl.pallas_call(
        functools.partial(_ffn_kernel, final_norm=final_norm),
        grid=(b, s // tm, nf),
        in_specs=[
            pl.BlockSpec((1, tm, d), lambda bi, i, f: (bi, i, 0)),
            pl.BlockSpec((1, d), lambda bi, i, f: (0, 0)),
            pl.BlockSpec((1, d), lambda bi, i, f: (0, 0)),
            pl.BlockSpec((d, tf), lambda bi, i, f: (0, f)),
            pl.BlockSpec((d, tf), lambda bi, i, f: (0, nf + f)),
            pl.BlockSpec((tf, d), lambda bi, i, f: (f, 0)),
        ],
        out_specs=pl.BlockSpec((1, tm, d), lambda bi, i, f: (bi, i, 0)),
        out_shape=jax.ShapeDtypeStruct((b, s, d), F32),
        scratch_shapes=[pltpu.VMEM((tm, d), BF16)],
        compiler_params=_params(("parallel", "parallel", "arbitrary")),
        name="ffn",
    )(x, g.reshape(1, d), g_final.reshape(1, d), w_gate_up, w_gate_up, w_down)


def _trunk(x, mem, p):
    for i in range(2):
        kv = _norm_proj(mem.reshape(1, -1, mem.shape[-1]), p["norm_mem"][i], p["w_mem_kv"][i],
                        head_w=HEAD_DIM, tm=mem.shape[0] * N_MEM)
        if i == 0:
            projs = _norm_proj(x, p["norm_mix"][i], p["a_w_in"], head_w=HEAD_DIM, tm=TM_PROJ, tn=TN_A,
                               segments=p["a_segments"], q_steps=A_SLOTS * HEAD_DIM // TN_A, q_scale=ATT_Q_SCALE)
            mix = _dilated_attention(*projs)
            proj, q_head0 = projs[0], 3 * A_SLOTS
            w1, w2 = p["a_w_out1"], p["a_w_out2"]
        else:
            proj = _norm_proj(x, p["norm_mix"][i], p["b_w_in"], head_w=RET_DK, tm=TM_PROJ_B, hold=2,
                              vmem_limit=PROJ_B_VMEM_LIMIT)
            q_head0 = 4 * RET_HEADS
            mix = _retention(proj, p["b_decay"])
            w1, w2 = p["b_w_out1"], p["b_w_out2"]
        x = _out_proj(x, mix, proj, q_head0, kv, w1, w2, tm=TM_OUT)
        x = _ffn(x, p["norm_ffn"][i], p["norm_final"], p["w_gate_up"][i], p["w_down"][i],
                 final_norm=(i == 1), tm=TM_FFN)
    return x


def _group_columns(w_in):
    gw = A_SLOTS * HEAD_DIM
    cols, segments = [], []
    for g, (_, dil) in enumerate(DIL_GROUPS):
        group = [w_in[:, (c * N_GROUPS + g) * gw:(c * N_GROUPS + g + 1) * gw] for c in range(3)]
        if g == 0:
            group.append(w_in[:, 3 * N_GROUPS * gw:])
        width = sum(w.shape[1] for w in group)
        pad = -width % TN_A
        if pad:
            group.append(jnp.zeros((w_in.shape[0], pad), w_in.dtype))
        cols += group
        segments.append(((width + pad) // TN_A, dil))
    return jnp.concatenate(cols, axis=1), tuple(segments)


def kernel(x_prompt, x_sample, mem_prompt, mem_sample, norm_mix, norm_mem, w_mem_kv, a_w_in, a_w_out,
           b_w_in, b_w_out, b_decay_fwd, b_decay_bwd, norm_ffn, w_gate_up, w_down, norm_final):
    a_mix = A_SLOTS * HEAD_DIM
    b_mix = RET_HEADS * RET_DK
    assert x_prompt.shape[-1] == D_MODEL and x_sample.shape[-1] == D_MODEL
    assert mem_prompt.shape[1:] == (N_MEM, D_MODEL) and mem_sample.shape[1:] == (N_MEM, D_MODEL)
    assert norm_mix.shape[0] == 2 and a_w_in.shape[0] == 1 and b_w_in.shape[0] == 1
    a_cols, a_segments = _group_columns(a_w_in[0])
    p = {
        "norm_mix": norm_mix, "norm_mem": norm_mem, "norm_ffn": norm_ffn, "norm_final": norm_final,
        "w_mem_kv": w_mem_kv.astype(BF16),
        "a_w_in": a_cols.astype(BF16), "a_segments": a_segments,
        "a_w_out1": a_w_out[0, :a_mix].astype(BF16), "a_w_out2": a_w_out[0, a_mix:].astype(BF16),
        "b_w_in": b_w_in[0].astype(BF16),
        "b_w_out1": b_w_out[0, :b_mix].astype(BF16), "b_w_out2": b_w_out[0, b_mix:].astype(BF16),
        "b_decay": jnp.stack([b_decay_fwd[0], b_decay_bwd[0]]).astype(F32),
        "w_gate_up": w_gate_up.astype(BF16), "w_down": w_down.astype(BF16),
    }
    return _trunk(x_prompt, mem_prompt, p), _trunk(x_sample, mem_sample, p)
```

```python
import functools

import jax
import jax.numpy as jnp
from jax import lax
from jax.experimental import pallas as pl
from jax.experimental.pallas import tpu as pltpu

F32 = jnp.float32
BF16 = jnp.bfloat16

D_MODEL = 2048
HEAD_DIM = 128
DIL_GROUPS = ((128, 1), (512, 4), (2048, 16))
N_GROUPS = len(DIL_GROUPS)
A_SLOTS = D_MODEL // 256
A_HEADS = N_GROUPS * A_SLOTS
RET_HEADS = D_MODEL // 256
RET_DK = 256
RET_CHUNK = 256
CROSS_HEADS = 4
CROSS_DIM = CROSS_HEADS * HEAD_DIM
N_MEM = 256
D_FF = 5632
EPS = 1e-6
NEG_BIG = -1e30
LOG2_E = 1.4426950408889634
ATT_Q_SCALE = HEAD_DIM ** -0.5 * LOG2_E

ATT_BLOCK = 128
ATT_HALF = 64
ATT_KEYS = 256
ATT_BATCH = 8
DIL_RATIO = 4

TM_PROJ = 1024
TM_PROJ_B = 2048
TM_OUT = 512
TM_FFN = 1024
TN = 512
TN_A = 1024

V7X_VMEM_LIMIT = 56 * 1024 * 1024
MXU_COLS = 256


def _params(semantics):
    return pltpu.CompilerParams(dimension_semantics=semantics, vmem_limit_bytes=V7X_VMEM_LIMIT)


def _rmsnorm_f32(x, g):
    ms = jnp.mean(x * x, axis=-1, keepdims=True)
    return (x * lax.rsqrt(ms + EPS)) * g


def _norm_proj_kernel(x_ref, g_ref, w_ref, *rest, segments, head_w, half_w, q_steps, q_scale, pad_slabs):
    n_seg = len(segments)
    o_refs, xn_ref, scratch = rest[:n_seg], rest[n_seg], rest[n_seg + 1:]
    j = pl.program_id(2)
    tm, tn = xn_ref.shape[0], w_ref.shape[-1]
    per_half = half_w // head_w

    @pl.when(j == 0)
    def _():
        xn_ref[...] = _rmsnorm_f32(x_ref[0], g_ref[...]).astype(BF16)

    def project(half):
        return jnp.dot(xn_ref[...], w_ref[:, half * half_w:(half + 1) * half_w], preferred_element_type=F32)

    def deinterleave(o_ref, h, dil):
        stage_ref = scratch[0]
        if dil <= DIL_RATIO:
            for r in range(dil):
                o_ref[0, h, r] = stage_ref[h, pl.ds(r, tm // dil, stride=dil), :].astype(o_ref.dtype)
            return
        stage2_ref = scratch[1]
        outer = dil // DIL_RATIO
        for r1 in range(DIL_RATIO):
            stage2_ref[h, r1] = stage_ref[h, pl.ds(r1, tm // DIL_RATIO, stride=DIL_RATIO), :]
        for r1 in range(DIL_RATIO):
            for k in range(outer):
                o_ref[0, h, k * DIL_RATIO + r1] = (
                    stage2_ref[h, r1, pl.ds(k, tm // dil, stride=outer), :].astype(o_ref.dtype))

    n_slabs = tn // half_w
    start = 0
    for seg, (o_ref, (steps, dil)) in enumerate(zip(o_refs, segments)):
        def segment(o_ref=o_ref, dil=dil, start=start, live_slabs=n_slabs):
            for half in range(live_slabs):
                res = project(half)
                if q_steps:
                    res = res * jnp.where(j - start < q_steps, q_scale, 1.0)
                for hh in range(per_half):
                    h = half * per_half + hh
                    cols = res[:, hh * head_w:(hh + 1) * head_w]
                    if dil == 1:
                        o_ref[0, h] = cols.astype(o_ref.dtype)
                    else:
                        scratch[0][h] = cols
                if dil > 1:
                    for hh in range(per_half):
                        deinterleave(o_ref, half * per_half + hh, dil)
            for h in range(live_slabs * per_half, n_slabs * per_half):
                o_ref[0, h] = jnp.zeros(o_ref.shape[2:], o_ref.dtype)

        if n_seg == 1:
            segment()
        elif seg == 0 and pad_slabs:
            assert dil == 1
            pl.when((j >= start) & (j < start + steps - 1))(segment)
            pl.when(j == start + steps - 1)(functools.partial(segment, live_slabs=n_slabs - pad_slabs))
        else:
            pl.when((j >= start) & (j < start + steps))(segment)
        start += steps


def _norm_proj(x, g, w, *, head_w, tm, tn=TN, segments=None, q_steps=0, q_scale=1.0, pad_cols=0):
    b, s, d = x.shape
    nt = w.shape[1] // tn
    hps = tn // head_w
    segments = segments or ((nt, 1),)
    max_dil = max(dil for _, dil in segments)
    assert s % tm == 0 and w.shape == (d, nt * tn) and tn % max(head_w, MXU_COLS) == 0
    assert sum(steps for steps, _ in segments) == nt
    assert tm % (max_dil * 16) == 0
    scratch = [pltpu.VMEM((tm, d), BF16)]
    if max_dil > 1:
        scratch.append(pltpu.VMEM((hps, tm, head_w), F32))
    if max_dil > DIL_RATIO:
        scratch.append(pltpu.VMEM((hps, DIL_RATIO, tm // DIL_RATIO, head_w), F32))
    out_specs, out_shapes, start = [], [], 0
    for steps, dil in segments:
        def held(j, start=start, steps=steps):
            return jnp.clip(j - start, 0, steps - 1)
        if dil == 1:
            out_specs.append(pl.BlockSpec((1, hps, tm, head_w), lambda bi, i, j, held=held: (bi, held(j), i, 0)))
            out_shapes.append(jax.ShapeDtypeStruct((b, steps * hps, s, head_w), BF16))
        else:
            out_specs.append(pl.BlockSpec((1, hps, dil, tm // dil, head_w),
                                          lambda bi, i, j, held=held: (bi, held(j), 0, i, 0)))
            out_shapes.append(jax.ShapeDtypeStruct((b, steps * hps, dil, s // dil, head_w), BF16))
        start += steps
    kern = functools.partial(_norm_proj_kernel, segments=tuple(segments), head_w=head_w,
                             half_w=max(head_w, MXU_COLS), q_steps=q_steps, q_scale=q_scale,
                             pad_slabs=pad_cols // max(head_w, MXU_COLS))
    outs = pl.pallas_call(
        kern,
        grid=(b, s // tm, nt),
        in_specs=[
            pl.BlockSpec((1, tm, d), lambda bi, i, j: (bi, i, 0)),
            pl.BlockSpec((1, d), lambda bi, i, j: (0, 0)),
            pl.BlockSpec((d, tn), lambda bi, i, j: (0, j)),
        ],
        out_specs=out_specs,
        out_shape=out_shapes,
        scratch_shapes=scratch,
        compiler_params=_params(("parallel", "parallel", "arbitrary")),
        name="norm_proj",
    )(x, g.reshape(1, d), w)
    return outs if len(outs) > 1 else outs[0]


def _dil_attn_kernel(q0, k0, v0, q1, k1, v1, q2, k2, v2, o_ref,
                     m1_ref, l1_ref, a1_ref, m0_ref, l0_ref, a0_ref, bias0_ref, bias1_ref, bias2_ref,
                     *, seq, unroll):
    h = pl.program_id(0)
    groups = ((q0, k0, v0, bias0_ref), (q1, k1, v1, bias1_ref), (q2, k2, v2, bias2_ref))
    stat = (ATT_BLOCK, HEAD_DIM)

    def build_bias(g):
        bias_ref = groups[g][3]
        dil = DIL_GROUPS[g][1]
        kw = bias_ref.shape[-1]
        jf = (h + (g * A_SLOTS + 1)).astype(F32)
        slope = jnp.exp2(jnp.full((ATT_BLOCK, kw), -8.0 / A_HEADS, F32) * jf)
        row = lax.broadcasted_iota(jnp.int32, (ATT_BLOCK, kw), 0)
        col = lax.broadcasted_iota(jnp.int32, (ATT_BLOCK, kw), 1)
        for t, delta in enumerate((0, -ATT_HALF, ATT_BLOCK - kw)):
            arel = jnp.abs(col - row + delta)
            bias_ref[t] = jnp.where(arel <= ATT_HALF, (-slope * (dil * arel).astype(F32)) * LOG2_E, NEG_BIG)

    def scores(g, q, k, blk, nblk):
        bias_ref = groups[g][3]
        table = jnp.where(blk == 0, 0, jnp.where(blk == nblk - 1, 2, 1))
        return lax.dot_general(q, k, (((1,), (1,)), ((), ())), preferred_element_type=F32) + bias_ref[table]

    def accumulate(s, v, state):
        m_new = jnp.broadcast_to(jnp.max(s, axis=-1, keepdims=True), stat)
        if state is not None:
            m_old, l_old, acc_old = state
            m_new = jnp.maximum(m_old, m_new)
            alpha = jnp.exp2(m_old - m_new)
        slabs = [s[:, c:c + HEAD_DIM] - m_new for c in range(0, s.shape[1], HEAD_DIM)]
        p = jnp.exp2(slabs[0] if len(slabs) == 1 else jnp.concatenate(slabs, axis=1))
        l_new = jnp.broadcast_to(jnp.sum(p, axis=-1, keepdims=True), stat)
        acc_new = jnp.dot(p.astype(BF16), v, preferred_element_type=F32)
        if state is not None:
            l_new = alpha * l_old + l_new
            acc_new = alpha * acc_old + acc_new
        return m_new, l_new, acc_new

    def run_group(g):
        q_ref, k_ref, v_ref, bias_ref = groups[g]
        dil = DIL_GROUPS[g][1]
        length = seq // dil
        kw = bias_ref.shape[-1]
        nblk = length // ATT_BLOCK
        shift = nblk.bit_length() - 1

        def score_block(it):
            r = it >> shift
            blk = it & (nblk - 1)
            i0 = pl.multiple_of(blk * ATT_BLOCK, ATT_BLOCK)
            ws = pl.multiple_of(jnp.clip(i0 - ATT_HALF, 0, length - kw), ATT_HALF)
            if g == 0:
                q = q_ref[0, 0, pl.ds(i0, ATT_BLOCK), :]
                k = k_ref[0, 0, pl.ds(ws, kw), :]
                v = v_ref[0, 0, pl.ds(ws, kw), :]
            else:
                q = q_ref[0, 0, r, pl.ds(i0, ATT_BLOCK), :]
                k = k_ref[0, 0, r, pl.ds(ws, kw), :]
                v = v_ref[0, 0, r, pl.ds(ws, kw), :]
            return r, i0, scores(g, q, k, blk, nblk), v

        def finish_block(r, i0, s, v):
            if g == 2:
                m, l, acc = accumulate(s, v, None)
                rows = pl.ds(i0 * DIL_RATIO + (r >> 2), ATT_BLOCK, stride=DIL_RATIO)
                m1_ref[r & 3, rows, :] = m
                l1_ref[r & 3, rows, :] = l
                a1_ref[r & 3, rows, :] = acc
            elif g == 1:
                rows = pl.ds(i0, ATT_BLOCK)
                m, l, acc = accumulate(s, v, (m1_ref[r, rows, :], l1_ref[r, rows, :], a1_ref[r, rows, :]))
                rows = pl.ds(i0 * DIL_RATIO + r, ATT_BLOCK, stride=DIL_RATIO)
                m0_ref[rows, :] = m
                l0_ref[rows, :] = l
                a0_ref[rows, :] = acc
            else:
                rows = pl.ds(i0, ATT_BLOCK)
                _, l, acc = accumulate(s, v, (m0_ref[rows, :], l0_ref[rows, :], a0_ref[rows, :]))
                o_ref[0, rows, :] = (acc / l).astype(o_ref.dtype)

        total = dil * nblk
        batch = min(ATT_BATCH, total) if kw < ATT_KEYS else 1

        def body(bi, carry):
            blocks = [score_block(bi * batch + i) for i in range(batch)]
            for blk_args in blocks:
                finish_block(*blk_args)
            return carry

        lax.fori_loop(0, total // batch, body, 0, unroll=max(1, min(unroll, total) // batch))

    @pl.when(pl.program_id(1) == 0)
    def _():
        for g in range(N_GROUPS):
            build_bias(g)

    run_group(2)
    run_group(1)
    run_group(0)


def _dilated_attention(p0, p1, p2, *, unroll=32):
    b, _, s, hd = p0.shape
    n_far = s // (DIL_GROUPS[-1][1] * ATT_BLOCK)
    assert n_far >= 1 and s % (DIL_GROUPS[-1][1] * ATT_BLOCK) == 0 and n_far & (n_far - 1) == 0
    assert all(window // (2 * dil) == ATT_HALF for window, dil in DIL_GROUPS)
    assert all(DIL_GROUPS[g + 1][1] == DIL_RATIO * DIL_GROUPS[g][1] for g in range(N_GROUPS - 1))
    args, specs = [], []
    for c in range(3):
        args.append(p0)
        specs.append(pl.BlockSpec((1, 1, s, hd), lambda hi, bi, c=c: (bi, c * A_SLOTS + hi, 0, 0)))
    for pg in (p1, p2):
        dil, length = pg.shape[2], pg.shape[3]
        for c in range(3):
            args.append(pg)
            specs.append(pl.BlockSpec((1, 1, dil, length, hd), lambda hi, bi, c=c: (bi, c * A_SLOTS + hi, 0, 0, 0)))
    l1 = s // DIL_GROUPS[1][1]
    bias = [pltpu.VMEM((3, ATT_BLOCK, min(ATT_KEYS, s // dil)), F32) for _, dil in DIL_GROUPS]
    return pl.pallas_call(
        functools.partial(_dil_attn_kernel, seq=s, unroll=unroll),
        grid=(A_SLOTS, b),
        in_specs=specs,
        out_specs=pl.BlockSpec((1, s, hd), lambda hi, bi: (bi, 0, hi)),
        out_shape=jax.ShapeDtypeStruct((b, s, A_SLOTS * hd), BF16),
        scratch_shapes=[pltpu.VMEM((DIL_RATIO, l1, hd), F32)] * 3 + [pltpu.VMEM((s, hd), F32)] * 3 + bias,
        compiler_params=_params(("parallel", "arbitrary")),
        name="dilated_attention",
    )(*args)


def _retention_kernel(dec_ref, q_ref, k_ref, v_ref, gate_ref, o_ref,
                      yf_ref, yb_ref, sf_ref, sb_ref, qdf_ref, kdf_ref, qdb_ref, kdb_ref, intra_ref,
                      *, seq, unroll):
    h = pl.program_id(0)
    c = RET_CHUNK
    n = seq // c
    dk = RET_DK
    q_scale = dk ** -0.5

    def log_gamma(e, shape):
        return jnp.log1p(-jnp.exp2(-jnp.full(shape, e, F32)))

    e_f = dec_ref[0, h]
    e_b = dec_ref[1, h]
    cd_f = jnp.exp(c * log_gamma(e_f, (1, dk)))
    cd_b = jnp.exp(c * log_gamma(e_b, (1, dk)))

    @pl.when(pl.program_id(1) == 0)
    def _():
        lg_f = log_gamma(e_f, (c, dk))
        lg_b = log_gamma(e_b, (c, dk))
        i_row = lax.broadcasted_iota(jnp.int32, (c, dk), 0).astype(F32)
        qdf_ref[...] = jnp.exp((i_row + 1.0) * lg_f) * q_scale
        kdf_ref[...] = jnp.exp((c - 1.0 - i_row) * lg_f)
        qdb_ref[...] = jnp.exp((c - i_row) * lg_b) * q_scale
        kdb_ref[...] = jnp.exp(i_row * lg_b)
        t = lax.broadcasted_iota(jnp.int32, (c, c), 0)
        s = lax.broadcasted_iota(jnp.int32, (c, c), 1)
        diff = (t - s).astype(F32)
        intra_ref[...] = q_scale * jnp.where(
            t >= s, jnp.exp(jnp.where(t >= s, diff, 0.0) * log_gamma(e_f, (c, c))),
            jnp.exp(jnp.where(t < s, -diff, 0.0) * log_gamma(e_b, (c, c))))

    def load(ci):
        rows = pl.ds(pl.multiple_of(ci * c, c), c)
        q = q_ref[0, 0, rows, :].astype(F32)
        k = k_ref[0, 0, rows, :].astype(F32)
        v = v_ref[0, 0, rows, :]
        return rows, q, k, v

    def inter(q_dec, k_dec, v, st_ref, cd):
        y = jnp.dot(q_dec.astype(BF16), st_ref[...].astype(BF16), preferred_element_type=F32)
        kv = lax.dot_general(k_dec.astype(BF16), v, (((0,), (0,)), ((), ())), preferred_element_type=F32)
        st_ref[...] = cd * st_ref[...] + kv
        return y

    sf_ref[...] = jnp.zeros_like(sf_ref)
    sb_ref[...] = jnp.zeros_like(sb_ref)

    def finish(rows, y):
        y = y * lax.rsqrt(jnp.mean(y * y, axis=-1, keepdims=True) + EPS)
        gate = gate_ref[0, 0, rows, :].astype(F32)
        o_ref[0, rows, :] = (y * (gate * jax.nn.sigmoid(gate))).astype(o_ref.dtype)

    half = n // 2

    def step(ci, carry, *, second_half):
        cj = n - 1 - ci
        rows, q, k, v = load(ci)
        sc = lax.dot_general(q_ref[0, 0, rows, :], k_ref[0, 0, rows, :], (((1,), (1,)), ((), ())),
                             preferred_element_type=F32) * intra_ref[...]
        y_f = jnp.dot(sc.astype(BF16), v, preferred_element_type=F32)
        y_f = y_f + inter(q * qdf_ref[...], k * kdf_ref[...], v, sf_ref, cd_f)
        if second_half:
            finish(rows, y_f + yb_ref[pl.ds(pl.multiple_of((ci - half) * c, c), c), :])
        else:
            yf_ref[rows, :] = y_f
        rows, q, k, v = load(cj)
        y_b = inter(q * qdb_ref[...], k * kdb_ref[...], v, sb_ref, cd_b)
        if second_half:
            finish(rows, yf_ref[rows, :] + y_b)
        else:
            yb_ref[pl.ds(pl.multiple_of((cj - half) * c, c), c), :] = y_b
        return carry

    lax.fori_loop(0, half, functools.partial(step, second_half=False), 0, unroll=min(unroll, half))
    lax.fori_loop(half, n, functools.partial(step, second_half=True), 0, unroll=min(unroll, half))


def _retention(proj, decays, *, unroll=8):
    b, _, s, dk = proj.shape
    c = RET_CHUNK
    assert dk == RET_DK and s % (2 * c) == 0

    def head_spec(offset):
        return pl.BlockSpec((1, 1, s, dk), lambda hi, bi: (bi, offset + hi, 0, 0))

    return pl.pallas_call(
        functools.partial(_retention_kernel, seq=s, unroll=unroll),
        grid=(RET_HEADS, b),
        in_specs=[pl.BlockSpec(memory_space=pltpu.SMEM)] + [head_spec(RET_HEADS * i) for i in range(4)],
        out_specs=pl.BlockSpec((1, s, dk), lambda hi, bi: (bi, 0, hi)),
        out_shape=jax.ShapeDtypeStruct((b, s, RET_HEADS * dk), BF16),
        scratch_shapes=[pltpu.VMEM((s // 2, dk), F32)] * 2 + [pltpu.VMEM((dk, dk), F32)] * 2
        + [pltpu.VMEM((c, dk), F32)] * 4 + [pltpu.VMEM((c, c), F32)],
        compiler_params=_params(("parallel", "arbitrary")),
        name="retention",
    )(decays, proj, proj, proj, proj)


def _out_proj_kernel(x_ref, mix_ref, q_ref, kv_ref, w1_ref, w2_ref, o_ref, *, q_w):
    scale = HEAD_DIM ** -0.5
    per = q_w // HEAD_DIM
    scores = []
    for hh in range(CROSS_HEADS):
        q = q_ref[0, hh // per][:, (hh % per) * HEAD_DIM:(hh % per + 1) * HEAD_DIM]
        scores.append(lax.dot_general(q, kv_ref[0, hh], (((1,), (1,)), ((), ())),
                                      preferred_element_type=F32) * scale)
    acc = jnp.dot(mix_ref[0], w1_ref[...], preferred_element_type=F32)
    outs = []
    for hh, s in enumerate(scores):
        m = jnp.max(s, axis=-1, keepdims=True)
        p = jnp.exp(s - m)
        l = jnp.sum(p, axis=-1, keepdims=True)
        o = jnp.dot(p.astype(BF16), kv_ref[0, CROSS_HEADS + hh], preferred_element_type=F32) / l
        outs.append(o.astype(BF16))
    cross = jnp.concatenate(outs, axis=1)
    acc = acc + jnp.dot(cross, w2_ref[...], preferred_element_type=F32)
    o_ref[0] = x_ref[0] + acc


def _out_proj(x, mix, proj, q_head0, kv, w1, w2, *, tm):
    b, s, d = x.shape
    km = mix.shape[-1]
    q_w = proj.shape[-1]
    nq = CROSS_DIM // q_w
    assert s % tm == 0 and q_head0 % nq == 0 and w1.shape == (km, d) and w2.shape == (CROSS_DIM, d)
    return pl.pallas_call(
        functools.partial(_out_proj_kernel, q_w=q_w),
        grid=(b, s // tm),
        in_specs=[
            pl.BlockSpec((1, tm, d), lambda bi, i: (bi, i, 0)),
            pl.BlockSpec((1, tm, km), lambda bi, i: (bi, i, 0)),
            pl.BlockSpec((1, nq, tm, q_w), lambda bi, i: (bi, q_head0 // nq, i, 0)),
            pl.BlockSpec((1, 2 * CROSS_HEADS, N_MEM, HEAD_DIM), lambda bi, i: (0, 0, bi, 0)),
            pl.BlockSpec((km, d), lambda bi, i: (0, 0)),
            pl.BlockSpec((CROSS_DIM, d), lambda bi, i: (0, 0)),
        ],
        out_specs=pl.BlockSpec((1, tm, d), lambda bi, i: (bi, i, 0)),
        out_shape=jax.ShapeDtypeStruct((b, s, d), F32),
        compiler_params=_params(("parallel", "parallel")),
        name="out_proj",
    )(x, mix, proj, kv, w1, w2)


def _ffn_kernel(x_ref, g_ref, gf_ref, wg_ref, wu_ref, wd_ref, o_ref, xn_ref, *, final_norm):
    f = pl.program_id(2)

    @pl.when(f == 0)
    def _():
        x = x_ref[0]
        xn_ref[...] = _rmsnorm_f32(x, g_ref[...]).astype(BF16)
        o_ref[0] = x

    xn = xn_ref[...]
    acts = []
    for c in range(0, wg_ref.shape[1], MXU_COLS):
        gate = jnp.dot(xn, wg_ref[:, c:c + MXU_COLS], preferred_element_type=F32)
        up = jnp.dot(xn, wu_ref[:, c:c + MXU_COLS], preferred_element_type=F32)
        acts.append(((gate * jax.nn.sigmoid(gate)) * up).astype(BF16))
    o_ref[0] += jnp.dot(jnp.concatenate(acts, axis=1), wd_ref[...], preferred_element_type=F32)

    if final_norm:
        @pl.when(f == pl.num_programs(2) - 1)
        def _():
            o_ref[0] = _rmsnorm_f32(o_ref[0], gf_ref[...])


def _ffn(x, g, g_final, w_gate_up, w_down, *, final_norm, tm, tf=TN):
    b, s, d = x.shape
    nf = w_down.shape[0] // tf
    assert s % tm == 0 and w_down.shape == (nf * tf, d) and w_gate_up.shape == (d, 2 * nf * tf)
    return pl.pallas_call(
        functools.partial(_ffn_kernel, final_norm=final_norm),
        grid=(b, s // tm, nf),
        in_specs=[
            pl.BlockSpec((1, tm, d), lambda bi, i, f: (bi, i, 0)),
            pl.BlockSpec((1, d), lambda bi, i, f: (0, 0)),
            pl.BlockSpec((1, d), lambda bi, i, f: (0, 0)),
            pl.BlockSpec((d, tf), lambda bi, i, f: (0, f)),
            pl.BlockSpec((d, tf), lambda bi, i, f: (0, nf + f)),
            pl.BlockSpec((tf, d), lambda bi, i, f: (f, 0)),
        ],
        out_specs=pl.BlockSpec((1, tm, d), lambda bi, i, f: (bi, i, 0)),
        out_shape=jax.ShapeDtypeStruct((b, s, d), F32),
        scratch_shapes=[pltpu.VMEM((tm, d), BF16)],
        compiler_params=_params(("parallel", "parallel", "arbitrary")),
        name="ffn",
    )(x, g.reshape(1, d), g_final.reshape(1, d), w_gate_up, w_gate_up, w_down)


def _trunk(x, mem, p):
    for i in range(2):
        kv = _norm_proj(mem.reshape(1, -1, mem.shape[-1]), p["norm_mem"][i], p["w_mem_kv"][i],
                        head_w=HEAD_DIM, tm=mem.shape[0] * N_MEM)
        if i == 0:
            projs = _norm_proj(x, p["norm_mix"][i], p["a_w_in"], head_w=HEAD_DIM, tm=TM_PROJ, tn=TN_A,
                               segments=p["a_segments"], q_steps=A_SLOTS * HEAD_DIM // TN_A, q_scale=ATT_Q_SCALE,
                               pad_cols=p["a_pad"])
            mix = _dilated_attention(*projs)
            proj, q_head0 = projs[0], 3 * A_SLOTS
            w1, w2 = p["a_w_out1"], p["a_w_out2"]
        else:
            proj, q_head0 = _norm_proj(x, p["norm_mix"][i], p["b_w_in"], head_w=RET_DK, tm=TM_PROJ_B), 4 * RET_HEADS
            mix = _retention(proj, p["b_decay"])
            w1, w2 = p["b_w_out1"], p["b_w_out2"]
        x = _out_proj(x, mix, proj, q_head0, kv, w1, w2, tm=TM_OUT)
        x = _ffn(x, p["norm_ffn"][i], p["norm_final"], p["w_gate_up"][i], p["w_down"][i],
                 final_norm=(i == 1), tm=TM_FFN)
    return x


def _group_columns(w_in):
    gw = A_SLOTS * HEAD_DIM
    cols, segments, pad0 = [], [], 0
    for g, (_, dil) in enumerate(DIL_GROUPS):
        group = [w_in[:, (c * N_GROUPS + g) * gw:(c * N_GROUPS + g + 1) * gw] for c in range(3)]
        if g == 0:
            group.append(w_in[:, 3 * N_GROUPS * gw:])
        width = sum(w.shape[1] for w in group)
        pad = -width % TN_A
        assert pad == 0 or g == 0
        if pad:
            group.append(jnp.zeros((w_in.shape[0], pad), w_in.dtype))
            pad0 = pad
        cols += group
        segments.append(((width + pad) // TN_A, dil))
    return jnp.concatenate(cols, axis=1), tuple(segments), pad0


def kernel(x_prompt, x_sample, mem_prompt, mem_sample, norm_mix, norm_mem, w_mem_kv, a_w_in, a_w_out,
           b_w_in, b_w_out, b_decay_fwd, b_decay_bwd, norm_ffn, w_gate_up, w_down, norm_final):
    a_mix = A_SLOTS * HEAD_DIM
    b_mix = RET_HEADS * RET_DK
    assert x_prompt.shape[-1] == D_MODEL and x_sample.shape[-1] == D_MODEL
    assert mem_prompt.shape[1:] == (N_MEM, D_MODEL) and mem_sample.shape[1:] == (N_MEM, D_MODEL)
    assert norm_mix.shape[0] == 2 and a_w_in.shape[0] == 1 and b_w_in.shape[0] == 1
    a_cols, a_segments, a_pad = _group_columns(a_w_in[0])
    p = {
        "norm_mix": norm_mix, "norm_mem": norm_mem, "norm_ffn": norm_ffn, "norm_final": norm_final,
        "w_mem_kv": w_mem_kv.astype(BF16),
        "a_w_in": a_cols.astype(BF16), "a_segments": a_segments, "a_pad": a_pad,
        "a_w_out1": a_w_out[0, :a_mix].astype(BF16), "a_w_out2": a_w_out[0, a_mix:].astype(BF16),
        "b_w_in": b_w_in[0].astype(BF16),
        "b_w_out1": b_w_out[0, :b_mix].astype(BF16), "b_w_out2": b_w_out[0, b_mix:].astype(BF16),
        "b_decay": jnp.stack([b_decay_fwd[0], b_decay_bwd[0]]).astype(F32),
        "w_gate_up": w_gate_up.astype(BF16), "w_down": w_down.astype(BF16),
    }
    return _trunk(x_prompt, mem_prompt, p), _trunk(x_sample, mem_sample, p)
```

```python
import functools

import jax
import jax.numpy as jnp
from jax import lax
from jax.experimental import pallas as pl
from jax.experimental.pallas import tpu as pltpu

F32 = jnp.float32
BF16 = jnp.bfloat16

D_MODEL = 2048
HEAD_DIM = 128
DIL_GROUPS = ((128, 1), (512, 4), (2048, 16))
N_GROUPS = len(DIL_GROUPS)
A_SLOTS = D_MODEL // 256
A_HEADS = N_GROUPS * A_SLOTS
RET_HEADS = D_MODEL // 256
RET_DK = 256
RET_CHUNK = 256
CROSS_HEADS = 4
CROSS_DIM = CROSS_HEADS * HEAD_DIM
N_MEM = 256
D_FF = 5632
EPS = 1e-6
NEG_BIG = -1e30
LOG2_E = 1.4426950408889634
ATT_Q_SCALE = HEAD_DIM ** -0.5 * LOG2_E

ATT_BLOCK = 128
ATT_HALF = 64
ATT_KEYS = 256
ATT_BATCH = 8
DIL_RATIO = 4

TM_PROJ = 1024
TM_PROJ_B = 2048
TM_OUT = 512
TM_FFN = 1024
TN = 512
TN_A = 1024

V7X_VMEM_LIMIT = 56 * 1024 * 1024
PROJ_B_VMEM_LIMIT = 62 * 1024 * 1024
MXU_COLS = 256


def _params(semantics):
    return pltpu.CompilerParams(dimension_semantics=semantics, vmem_limit_bytes=V7X_VMEM_LIMIT)


def _rmsnorm_f32(x, g):
    ms = jnp.mean(x * x, axis=-1, keepdims=True)
    return (x * lax.rsqrt(ms + EPS)) * g


def _norm_proj_kernel(x_ref, g_ref, w_ref, *rest, segments, head_w, half_w, q_steps, q_scale, pad_slabs):
    n_seg = len(segments)
    o_refs, xn_ref, scratch = rest[:n_seg], rest[n_seg], rest[n_seg + 1:]
    j = pl.program_id(2)
    tm, tn = xn_ref.shape[0], w_ref.shape[-1]
    per_half = half_w // head_w

    @pl.when(j == 0)
    def _():
        xn_ref[...] = _rmsnorm_f32(x_ref[0], g_ref[...]).astype(BF16)

    def project(half):
        return jnp.dot(xn_ref[...], w_ref[:, half * half_w:(half + 1) * half_w], preferred_element_type=F32)

    def deinterleave(o_ref, h, dil):
        stage_ref = scratch[0]
        if dil <= DIL_RATIO:
            for r in range(dil):
                o_ref[0, h, r] = stage_ref[h, pl.ds(r, tm // dil, stride=dil), :].astype(o_ref.dtype)
            return
        stage2_ref = scratch[1]
        outer = dil // DIL_RATIO
        for r1 in range(DIL_RATIO):
            stage2_ref[h, r1] = stage_ref[h, pl.ds(r1, tm // DIL_RATIO, stride=DIL_RATIO), :]
        for r1 in range(DIL_RATIO):
            for k in range(outer):
                o_ref[0, h, k * DIL_RATIO + r1] = (
                    stage2_ref[h, r1, pl.ds(k, tm // dil, stride=outer), :].astype(o_ref.dtype))

    n_slabs = tn // half_w
    start = 0
    for seg, (o_ref, (steps, dil)) in enumerate(zip(o_refs, segments)):
        def segment(o_ref=o_ref, dil=dil, start=start, live_slabs=n_slabs):
            for half in range(live_slabs):
                res = project(half)
                if q_steps:
                    res = res * jnp.where(j - start < q_steps, q_scale, 1.0)
                for hh in range(per_half):
                    h = half * per_half + hh
                    cols = res[:, hh * head_w:(hh + 1) * head_w]
                    if dil == 1:
                        o_ref[0, h] = cols.astype(o_ref.dtype)
                    else:
                        scratch[0][h] = cols
                if dil > 1:
                    for hh in range(per_half):
                        deinterleave(o_ref, half * per_half + hh, dil)
            for h in range(live_slabs * per_half, n_slabs * per_half):
                o_ref[0, h] = jnp.zeros(o_ref.shape[2:], o_ref.dtype)

        if seg == 0 and pad_slabs:
            assert dil == 1
            pl.when((j >= start) & (j < start + steps - 1))(segment)
            pl.when(j == start + steps - 1)(functools.partial(segment, live_slabs=n_slabs - pad_slabs))
        elif n_seg == 1:
            segment()
        else:
            pl.when((j >= start) & (j < start + steps))(segment)
        start += steps


def _norm_proj(x, g, w, *, head_w, tm, tn=TN, segments=None, q_steps=0, q_scale=1.0, pad_cols=0,
               vmem_limit=V7X_VMEM_LIMIT):
    b, s, d = x.shape
    nt = w.shape[1] // tn
    hps = tn // head_w
    segments = segments or ((nt, 1),)
    max_dil = max(dil for _, dil in segments)
    assert s % tm == 0 and w.shape == (d, nt * tn) and tn % max(head_w, MXU_COLS) == 0
    assert sum(steps for steps, _ in segments) == nt
    assert tm % (max_dil * 16) == 0
    scratch = [pltpu.VMEM((tm, d), BF16)]
    if max_dil > 1:
        scratch.append(pltpu.VMEM((hps, tm, head_w), F32))
    if max_dil > DIL_RATIO:
        scratch.append(pltpu.VMEM((hps, DIL_RATIO, tm // DIL_RATIO, head_w), F32))
    out_specs, out_shapes, start = [], [], 0
    for steps, dil in segments:
        def held(j, start=start, steps=steps):
            return jnp.clip(j - start, 0, steps - 1)
        if dil == 1:
            out_specs.append(pl.BlockSpec((1, hps, tm, head_w), lambda bi, i, j, held=held: (bi, held(j), i, 0)))
            out_shapes.append(jax.ShapeDtypeStruct((b, steps * hps, s, head_w), BF16))
        else:
            out_specs.append(pl.BlockSpec((1, hps, dil, tm // dil, head_w),
                                          lambda bi, i, j, held=held: (bi, held(j), 0, i, 0)))
            out_shapes.append(jax.ShapeDtypeStruct((b, steps * hps, dil, s // dil, head_w), BF16))
        start += steps
    kern = functools.partial(_norm_proj_kernel, segments=tuple(segments), head_w=head_w,
                             half_w=max(head_w, MXU_COLS), q_steps=q_steps, q_scale=q_scale,
                             pad_slabs=pad_cols // max(head_w, MXU_COLS))
    outs = pl.pallas_call(
        kern,
        grid=(b, s // tm, nt),
        in_specs=[
            pl.BlockSpec((1, tm, d), lambda bi, i, j: (bi, i, 0)),
            pl.BlockSpec((1, d), lambda bi, i, j: (0, 0)),
            pl.BlockSpec((d, tn), lambda bi, i, j: (0, j)),
        ],
        out_specs=out_specs,
        out_shape=out_shapes,
        scratch_shapes=scratch,
        compiler_params=pltpu.CompilerParams(dimension_semantics=("parallel", "parallel", "arbitrary"),
                                             vmem_limit_bytes=vmem_limit),
        name="norm_proj",
    )(x, g.reshape(1, d), w)
    return outs if len(outs) > 1 else outs[0]


def _dil_attn_kernel(q0, k0, v0, q1, k1, v1, q2, k2, v2, o_ref,
                     m1_ref, l1_ref, a1_ref, m0_ref, l0_ref, a0_ref, bias0_ref, bias1_ref, bias2_ref,
                     *, seq, unroll):
    h = pl.program_id(0)
    groups = ((q0, k0, v0, bias0_ref), (q1, k1, v1, bias1_ref), (q2, k2, v2, bias2_ref))
    stat = (ATT_BLOCK, HEAD_DIM)

    def build_bias(g):
        bias_ref = groups[g][3]
        dil = DIL_GROUPS[g][1]
        kw = bias_ref.shape[-1]
        jf = (h + (g * A_SLOTS + 1)).astype(F32)
        slope = jnp.exp2(jnp.full((ATT_BLOCK, kw), -8.0 / A_HEADS, F32) * jf)
        row = lax.broadcasted_iota(jnp.int32, (ATT_BLOCK, kw), 0)
        col = lax.broadcasted_iota(jnp.int32, (ATT_BLOCK, kw), 1)
        for t, delta in enumerate((0, -ATT_HALF, ATT_BLOCK - kw)):
            arel = jnp.abs(col - row + delta)
            bias_ref[t] = jnp.where(arel <= ATT_HALF, (-slope * (dil * arel).astype(F32)) * LOG2_E, NEG_BIG)

    def scores(g, q, k, blk, nblk):
        bias_ref = groups[g][3]
        table = jnp.where(blk == 0, 0, jnp.where(blk == nblk - 1, 2, 1))
        return lax.dot_general(q, k, (((1,), (1,)), ((), ())), preferred_element_type=F32) + bias_ref[table]

    def accumulate(s, v, state):
        m_new = jnp.broadcast_to(jnp.max(s, axis=-1, keepdims=True), stat)
        if state is not None:
            m_old, l_old, acc_old = state
            m_new = jnp.maximum(m_old, m_new)
            alpha = jnp.exp2(m_old - m_new)
        slabs = [s[:, c:c + HEAD_DIM] - m_new for c in range(0, s.shape[1], HEAD_DIM)]
        p = jnp.exp2(slabs[0] if len(slabs) == 1 else jnp.concatenate(slabs, axis=1))
        l_new = jnp.broadcast_to(jnp.sum(p, axis=-1, keepdims=True), stat)
        acc_new = jnp.dot(p.astype(BF16), v, preferred_element_type=F32)
        if state is not None:
            l_new = alpha * l_old + l_new
            acc_new = alpha * acc_old + acc_new
        return m_new, l_new, acc_new

    def run_group(g):
        q_ref, k_ref, v_ref, bias_ref = groups[g]
        dil = DIL_GROUPS[g][1]
        length = seq // dil
        kw = bias_ref.shape[-1]
        nblk = length // ATT_BLOCK
        shift = nblk.bit_length() - 1

        def score_block(it):
            r = it >> shift
            blk = it & (nblk - 1)
            i0 = pl.multiple_of(blk * ATT_BLOCK, ATT_BLOCK)
            ws = pl.multiple_of(jnp.clip(i0 - ATT_HALF, 0, length - kw), ATT_HALF)
            if g == 0:
                q = q_ref[0, 0, pl.ds(i0, ATT_BLOCK), :]
                k = k_ref[0, 0, pl.ds(ws, kw), :]
                v = v_ref[0, 0, pl.ds(ws, kw), :]
            else:
                q = q_ref[0, 0, r, pl.ds(i0, ATT_BLOCK), :]
                k = k_ref[0, 0, r, pl.ds(ws, kw), :]
                v = v_ref[0, 0, r, pl.ds(ws, kw), :]
            return r, i0, scores(g, q, k, blk, nblk), v

        def finish_block(r, i0, s, v):
            if g == 2:
                m, l, acc = accumulate(s, v, None)
                rows = pl.ds(i0 * DIL_RATIO + (r >> 2), ATT_BLOCK, stride=DIL_RATIO)
                m1_ref[r & 3, rows, :] = m
                l1_ref[r & 3, rows, :] = l
                a1_ref[r & 3, rows, :] = acc
            elif g == 1:
                rows = pl.ds(i0, ATT_BLOCK)
                m, l, acc = accumulate(s, v, (m1_ref[r, rows, :], l1_ref[r, rows, :], a1_ref[r, rows, :]))
                rows = pl.ds(i0 * DIL_RATIO + r, ATT_BLOCK, stride=DIL_RATIO)
                m0_ref[rows, :] = m
                l0_ref[rows, :] = l
                a0_ref[rows, :] = acc
            else:
                rows = pl.ds(i0, ATT_BLOCK)
                _, l, acc = accumulate(s, v, (m0_ref[rows, :], l0_ref[rows, :], a0_ref[rows, :]))
                o_ref[0, rows, :] = (acc / l).astype(o_ref.dtype)

        total = dil * nblk
        batch = min(ATT_BATCH, total) if kw < ATT_KEYS else 1

        def body(bi, carry):
            blocks = [score_block(bi * batch + i) for i in range(batch)]
            for blk_args in blocks:
                finish_block(*blk_args)
            return carry

        lax.fori_loop(0, total // batch, body, 0, unroll=max(1, min(unroll, total) // batch))

    @pl.when(pl.program_id(1) == 0)
    def _():
        for g in range(N_GROUPS):
            build_bias(g)

    run_group(2)
    run_group(1)
    run_group(0)


def _dilated_attention(p0, p1, p2, *, unroll=32):
    b, _, s, hd = p0.shape
    n_far = s // (DIL_GROUPS[-1][1] * ATT_BLOCK)
    assert n_far >= 1 and s % (DIL_GROUPS[-1][1] * ATT_BLOCK) == 0 and n_far & (n_far - 1) == 0
    assert all(window // (2 * dil) == ATT_HALF for window, dil in DIL_GROUPS)
    assert all(DIL_GROUPS[g + 1][1] == DIL_RATIO * DIL_GROUPS[g][1] for g in range(N_GROUPS - 1))
    args, specs = [], []
    for c in range(3):
        args.append(p0)
        specs.append(pl.BlockSpec((1, 1, s, hd), lambda hi, bi, c=c: (bi, c * A_SLOTS + hi, 0, 0)))
    for pg in (p1, p2):
        dil, length = pg.shape[2], pg.shape[3]
        for c in range(3):
            args.append(pg)
            specs.append(pl.BlockSpec((1, 1, dil, length, hd), lambda hi, bi, c=c: (bi, c * A_SLOTS + hi, 0, 0, 0)))
    l1 = s // DIL_GROUPS[1][1]
    bias = [pltpu.VMEM((3, ATT_BLOCK, min(ATT_KEYS, s // dil)), F32) for _, dil in DIL_GROUPS]
    return pl.pallas_call(
        functools.partial(_dil_attn_kernel, seq=s, unroll=unroll),
        grid=(A_SLOTS, b),
        in_specs=specs,
        out_specs=pl.BlockSpec((1, s, hd), lambda hi, bi: (bi, 0, hi)),
        out_shape=jax.ShapeDtypeStruct((b, s, A_SLOTS * hd), BF16),
        scratch_shapes=[pltpu.VMEM((DIL_RATIO, l1, hd), F32)] * 3 + [pltpu.VMEM((s, hd), F32)] * 3 + bias,
        compiler_params=_params(("parallel", "arbitrary")),
        name="dilated_attention",
    )(*args)


def _retention_kernel(dec_ref, q_ref, k_ref, v_ref, gate_ref, o_ref,
                      yf_ref, yb_ref, sf_ref, sb_ref, qdf_ref, kdf_ref, qdb_ref, kdb_ref, intra_ref,
                      *, seq, unroll):
    h = pl.program_id(0)
    c = RET_CHUNK
    n = seq // c
    dk = RET_DK
    q_scale = dk ** -0.5

    def log_gamma(e, shape):
        return jnp.log1p(-jnp.exp2(-jnp.full(shape, e, F32)))

    e_f = dec_ref[0, h]
    e_b = dec_ref[1, h]
    cd_f = jnp.exp(c * log_gamma(e_f, (1, dk)))
    cd_b = jnp.exp(c * log_gamma(e_b, (1, dk)))

    @pl.when(pl.program_id(1) == 0)
    def _():
        lg_f = log_gamma(e_f, (c, dk))
        lg_b = log_gamma(e_b, (c, dk))
        i_row = lax.broadcasted_iota(jnp.int32, (c, dk), 0).astype(F32)
        qdf_ref[...] = jnp.exp((i_row + 1.0) * lg_f) * q_scale
        kdf_ref[...] = jnp.exp((c - 1.0 - i_row) * lg_f)
        qdb_ref[...] = jnp.exp((c - i_row) * lg_b) * q_scale
        kdb_ref[...] = jnp.exp(i_row * lg_b)
        t = lax.broadcasted_iota(jnp.int32, (c, c), 0)
        s = lax.broadcasted_iota(jnp.int32, (c, c), 1)
        diff = (t - s).astype(F32)
        intra_ref[...] = q_scale * jnp.where(
            t >= s, jnp.exp(jnp.where(t >= s, diff, 0.0) * log_gamma(e_f, (c, c))),
            jnp.exp(jnp.where(t < s, -diff, 0.0) * log_gamma(e_b, (c, c))))

    def load(ci):
        rows = pl.ds(pl.multiple_of(ci * c, c), c)
        q = q_ref[0, 0, rows, :].astype(F32)
        k = k_ref[0, 0, rows, :].astype(F32)
        v = v_ref[0, 0, rows, :]
        return rows, q, k, v

    def inter(q_dec, k_dec, v, st_ref, cd):
        y = jnp.dot(q_dec.astype(BF16), st_ref[...].astype(BF16), preferred_element_type=F32)
        kv = lax.dot_general(k_dec.astype(BF16), v, (((0,), (0,)), ((), ())), preferred_element_type=F32)
        st_ref[...] = cd * st_ref[...] + kv
        return y

    sf_ref[...] = jnp.zeros_like(sf_ref)
    sb_ref[...] = jnp.zeros_like(sb_ref)

    def finish(rows, y):
        y = y * lax.rsqrt(jnp.mean(y * y, axis=-1, keepdims=True) + EPS)
        gate = gate_ref[0, 0, rows, :].astype(F32)
        o_ref[0, rows, :] = (y * (gate * jax.nn.sigmoid(gate))).astype(o_ref.dtype)

    half = n // 2

    def step(ci, carry, *, second_half):
        cj = n - 1 - ci
        rows, q, k, v = load(ci)
        sc = lax.dot_general(q_ref[0, 0, rows, :], k_ref[0, 0, rows, :], (((1,), (1,)), ((), ())),
                             preferred_element_type=F32) * intra_ref[...]
        y_f = jnp.dot(sc.astype(BF16), v, preferred_element_type=F32)
        y_f = y_f + inter(q * qdf_ref[...], k * kdf_ref[...], v, sf_ref, cd_f)
        if second_half:
            finish(rows, y_f + yb_ref[pl.ds(pl.multiple_of((ci - half) * c, c), c), :])
        else:
            yf_ref[rows, :] = y_f
        rows, q, k, v = load(cj)
        y_b = inter(q * qdb_ref[...], k * kdb_ref[...], v, sb_ref, cd_b)
        if second_half:
            finish(rows, yf_ref[rows, :] + y_b)
        else:
            yb_ref[pl.ds(pl.multiple_of((cj - half) * c, c), c), :] = y_b
        return carry

    lax.fori_loop(0, half, functools.partial(step, second_half=False), 0, unroll=min(unroll, half))
    lax.fori_loop(half, n, functools.partial(step, second_half=True), 0, unroll=min(unroll, half))


def _retention(proj, decays, *, unroll=8):
    b, _, s, dk = proj.shape
    c = RET_CHUNK
    assert dk == RET_DK and s % (2 * c) == 0

    def head_spec(offset):
        return pl.BlockSpec((1, 1, s, dk), lambda hi, bi: (bi, offset + hi, 0, 0))

    return pl.pallas_call(
        functools.partial(_retention_kernel, seq=s, unroll=unroll),
        grid=(RET_HEADS, b),
        in_specs=[pl.BlockSpec(memory_space=pltpu.SMEM)] + [head_spec(RET_HEADS * i) for i in range(4)],
        out_specs=pl.BlockSpec((1, s, dk), lambda hi, bi: (bi, 0, hi)),
        out_shape=jax.ShapeDtypeStruct((b, s, RET_HEADS * dk), BF16),
        scratch_shapes=[pltpu.VMEM((s // 2, dk), F32)] * 2 + [pltpu.VMEM((dk, dk), F32)] * 2
        + [pltpu.VMEM((c, dk), F32)] * 4 + [pltpu.VMEM((c, c), F32)],
        compiler_params=_params(("parallel", "arbitrary")),
        name="retention",
    )(decays, proj, proj, proj, proj)


def _out_proj_kernel(x_ref, mix_ref, q_ref, kv_ref, w1_ref, w2_ref, o_ref, *, q_w):
    scale = HEAD_DIM ** -0.5
    per = q_w // HEAD_DIM
    scores = []
    for hh in range(CROSS_HEADS):
        q = q_ref[0, hh // per][:, (hh % per) * HEAD_DIM:(hh % per + 1) * HEAD_DIM]
        scores.append(lax.dot_general(q, kv_ref[0, hh], (((1,), (1,)), ((), ())),
                                      preferred_element_type=F32) * scale)
    acc = jnp.dot(mix_ref[0], w1_ref[...], preferred_element_type=F32)
    outs = []
    for hh, s in enumerate(scores):
        m = jnp.max(s, axis=-1, keepdims=True)
        p = jnp.exp(s - m)
        l = jnp.sum(p, axis=-1, keepdims=True)
        o = jnp.dot(p.astype(BF16), kv_ref[0, CROSS_HEADS + hh], preferred_element_type=F32) / l
        outs.append(o.astype(BF16))
    cross = jnp.concatenate(outs, axis=1)
    acc = acc + jnp.dot(cross, w2_ref[...], preferred_element_type=F32)
    o_ref[0] = x_ref[0] + acc


def _out_proj(x, mix, proj, q_head0, kv, w1, w2, *, tm):
    b, s, d = x.shape
    km = mix.shape[-1]
    q_w = proj.shape[-1]
    nq = CROSS_DIM // q_w
    assert s % tm == 0 and q_head0 % nq == 0 and w1.shape == (km, d) and w2.shape == (CROSS_DIM, d)
    return pl.pallas_call(
        functools.partial(_out_proj_kernel, q_w=q_w),
        grid=(b, s // tm),
        in_specs=[
            pl.BlockSpec((1, tm, d), lambda bi, i: (bi, i, 0)),
            pl.BlockSpec((1, tm, km), lambda bi, i: (bi, i, 0)),
            pl.BlockSpec((1, nq, tm, q_w), lambda bi, i: (bi, q_head0 // nq, i, 0)),
            pl.BlockSpec((1, 2 * CROSS_HEADS, N_MEM, HEAD_DIM), lambda bi, i: (0, 0, bi, 0)),
            pl.BlockSpec((km, d), lambda bi, i: (0, 0)),
            pl.BlockSpec((CROSS_DIM, d), lambda bi, i: (0, 0)),
        ],
        out_specs=pl.BlockSpec((1, tm, d), lambda bi, i: (bi, i, 0)),
        out_shape=jax.ShapeDtypeStruct((b, s, d), F32),
        compiler_params=_params(("parallel", "parallel")),
        name="out_proj",
    )(x, mix, proj, kv, w1, w2)


def _ffn_kernel(x_ref, g_ref, gf_ref, wg_ref, wu_ref, wd_ref, o_ref, xn_ref, *, final_norm):
    f = pl.program_id(2)

    @pl.when(f == 0)
    def _():
        x = x_ref[0]
        xn_ref[...] = _rmsnorm_f32(x, g_ref[...]).astype(BF16)
        o_ref[0] = x

    xn = xn_ref[...]
    acts = []
    for c in range(0, wg_ref.shape[1], MXU_COLS):
        gate = jnp.dot(xn, wg_ref[:, c:c + MXU_COLS], preferred_element_type=F32)
        up = jnp.dot(xn, wu_ref[:, c:c + MXU_COLS], preferred_element_type=F32)
        acts.append(((gate * jax.nn.sigmoid(gate)) * up).astype(BF16))
    o_ref[0] += jnp.dot(jnp.concatenate(acts, axis=1), wd_ref[...], preferred_element_type=F32)

    if final_norm:
        @pl.when(f == pl.num_programs(2) - 1)
        def _():
            o_ref[0] = _rmsnorm_f32(o_ref[0], gf_ref[...])


def _ffn(x, g, g_final, w_gate_up, w_down, *, final_norm, tm, tf=TN):
    b, s, d = x.shape
    nf = w_down.shape[0] // tf
    assert s % tm == 0 and w_down.shape == (nf * tf, d) and w_gate_up.shape == (d, 2 * nf * tf)
    return pl.pallas_call(
        functools.partial(_ffn_kernel, final_norm=final_norm),
        grid=(b, s // tm, nf),
        in_specs=[
            pl.BlockSpec((1, tm, d), lambda bi, i, f: (bi, i, 0)),
            pl.BlockSpec((1, d), lambda bi, i, f: (0, 0)),
            pl.BlockSpec((1, d), lambda bi, i, f: (0, 0)),
            pl.BlockSpec((d, tf), lambda bi, i, f: (0, f)),
            pl.BlockSpec((d, tf), lambda bi, i, f: (0, nf + f)),
            pl.BlockSpec((tf, d), lambda bi, i, f: (f, 0)),
        ],
        out_specs=pl.BlockSpec((1, tm, d), lambda bi, i, f: (bi, i, 0)),
        out_shape=jax.ShapeDtypeStruct((b, s, d), F32),
        scratch_shapes=[pltpu.VMEM((tm, d), BF16)],
        compiler_params=_params(("parallel", "parallel", "arbitrary")),
        name="ffn",
    )(x, g.reshape(1, d), g_final.reshape(1, d), w_gate_up, w_gate_up, w_down)


def _trunk(x, mem, p):
    for i in range(2):
        kv = _norm_proj(mem.reshape(1, -1, mem.shape[-1]), p["norm_mem"][i], p["w_mem_kv"][i],
                        head_w=HEAD_DIM, tm=mem.shape[0] * N_MEM)
        if i == 0:
            projs = _norm_proj(x, p["norm_mix"][i], p["a_w_in"], head_w=HEAD_DIM, tm=TM_PROJ, tn=TN_A,
                               segments=p["a_segments"], q_steps=A_SLOTS * HEAD_DIM // TN_A, q_scale=ATT_Q_SCALE,
                               pad_cols=p["a_pad"])
            mix = _dilated_attention(*projs)
            proj, q_head0 = projs[0], 3 * A_SLOTS
            w1, w2 = p["a_w_out1"], p["a_w_out2"]
        else:
            proj = _norm_proj(x, p["norm_mix"][i], p["b_w_in"], head_w=RET_DK, tm=TM_PROJ_B, tn=TN_A,
                              pad_cols=p["b_pad"], vmem_limit=PROJ_B_VMEM_LIMIT)
            q_head0 = 4 * RET_HEADS
            mix = _retention(proj, p["b_decay"])
            w1, w2 = p["b_w_out1"], p["b_w_out2"]
        x = _out_proj(x, mix, proj, q_head0, kv, w1, w2, tm=TM_OUT)
        x = _ffn(x, p["norm_ffn"][i], p["norm_final"], p["w_gate_up"][i], p["w_down"][i],
                 final_norm=(i == 1), tm=TM_FFN)
    return x


def _group_columns(w_in):
    gw = A_SLOTS * HEAD_DIM
    cols, segments, pad0 = [], [], 0
    for g, (_, dil) in enumerate(DIL_GROUPS):
        group = [w_in[:, (c * N_GROUPS + g) * gw:(c * N_GROUPS + g + 1) * gw] for c in range(3)]
        if g == 0:
            group.append(w_in[:, 3 * N_GROUPS * gw:])
        width = sum(w.shape[1] for w in group)
        pad = -width % TN_A
        assert pad == 0 or g == 0
        if pad:
            group.append(jnp.zeros((w_in.shape[0], pad), w_in.dtype))
            pad0 = pad
        cols += group
        segments.append(((width + pad) // TN_A, dil))
    return jnp.concatenate(cols, axis=1), tuple(segments), pad0


def kernel(x_prompt, x_sample, mem_prompt, mem_sample, norm_mix, norm_mem, w_mem_kv, a_w_in, a_w_out,
           b_w_in, b_w_out, b_decay_fwd, b_decay_bwd, norm_ffn, w_gate_up, w_down, norm_final):
    a_mix = A_SLOTS * HEAD_DIM
    b_mix = RET_HEADS * RET_DK
    assert x_prompt.shape[-1] == D_MODEL and x_sample.shape[-1] == D_MODEL
    assert mem_prompt.shape[1:] == (N_MEM, D_MODEL) and mem_sample.shape[1:] == (N_MEM, D_MODEL)
    assert norm_mix.shape[0] == 2 and a_w_in.shape[0] == 1 and b_w_in.shape[0] == 1
    a_cols, a_segments, a_pad = _group_columns(a_w_in[0])
    b_pad = -b_w_in.shape[-1] % TN_A
    p = {
        "norm_mix": norm_mix, "norm_mem": norm_mem, "norm_ffn": norm_ffn, "norm_final": norm_final,
        "w_mem_kv": w_mem_kv.astype(BF16),
        "a_w_in": a_cols.astype(BF16), "a_segments": a_segments, "a_pad": a_pad,
        "a_w_out1": a_w_out[0, :a_mix].astype(BF16), "a_w_out2": a_w_out[0, a_mix:].astype(BF16),
        "b_w_in": jnp.pad(b_w_in[0], ((0, 0), (0, b_pad))).astype(BF16), "b_pad": b_pad,
        "b_w_out1": b_w_out[0, :b_mix].astype(BF16), "b_w_out2": b_w_out[0, b_mix:].astype(BF16),
        "b_decay": jnp.stack([b_decay_fwd[0], b_decay_bwd[0]]).astype(F32),
        "w_gate_up": w_gate_up.astype(BF16), "w_down": w_down.astype(BF16),
    }
    return _trunk(x_prompt, mem_prompt, p), _trunk(x_sample, mem_sample, p)
```
